```python
import math, functools
import jax, jax.numpy as jnp
from jax import lax
import numpy as np

D_MODEL = 2048
BATCH = 4
SEQ = 2048
DEPTH = 1
DEC_BATCH = 128
DEC_SEQ = 1
PAST_LEN = 16384
PAGE_SIZE = 128

N_META = 16
D_SSD = D_MODEL
SSD_HEADDIM = 64
SSD_HEADS = D_SSD // SSD_HEADDIM
SSD_GROUPS = 4
SSD_STATE = 128
SSD_CONV = 4
D_SC = D_MODEL
SC_CONV = 3
CHUNK = 128
EPS = 1e-5
D_GN = SSD_GROUPS * SSD_STATE
D_XBC = D_SSD + 2 * D_GN
D_MIX = D_SSD + D_SC
SPLITS = (D_SSD, D_SSD + D_XBC, D_SSD + D_XBC + SSD_HEADS,
          D_SSD + D_XBC + SSD_HEADS + D_SC, D_SSD + D_XBC + SSD_HEADS + 2 * D_SC,
          D_SSD + D_XBC + SSD_HEADS + 3 * D_SC)
D_IN_PROJ = D_SSD + D_XBC + SSD_HEADS + 4 * D_SC

kernel_name = "hymba_ssd_shortconv_step"


def rmsnorm(x, g):
    xf = x.astype(jnp.float32)
    y = xf * lax.rsqrt(jnp.mean(xf * xf, axis=-1, keepdims=True) + EPS)
    return (y * g.astype(jnp.float32)).astype(x.dtype)


def causal_dwconv(x_full, w):
    k = w.shape[0]
    t = x_full.shape[1] - k + 1
    out = x_full[:, 0:t] * w[0]
    for i in range(1, k):
        out = out + x_full[:, i:i + t] * w[i]
    return out


def ssd_chunked(x, dt, a, bm, cm):
    b, L = x.shape[0], x.shape[1]
    nc = L // CHUNK
    r = SSD_HEADS // SSD_GROUPS
    xf = x.astype(jnp.float32).reshape(b, nc, CHUNK, SSD_GROUPS, r, SSD_HEADDIM)
    dtf = dt.reshape(b, nc, CHUNK, SSD_GROUPS, r)
    bf = bm.astype(jnp.float32).reshape(b, nc, CHUNK, SSD_GROUPS, SSD_STATE)
    cf = cm.astype(jnp.float32).reshape(b, nc, CHUNK, SSD_GROUPS, SSD_STATE)
    a_cs = jnp.cumsum(dtf * a.reshape(SSD_GROUPS, r), axis=2)
    xdt = xf * dtf[..., None]
    seg = a_cs[:, :, :, None] - a_cs[:, :, None]
    causal = jnp.tril(jnp.ones((CHUNK, CHUNK), dtype=bool))[:, :, None, None]
    decay = jnp.exp(jnp.where(causal, seg, -jnp.inf))
    cb = jnp.einsum('bclgn,bcsgn->bclsg', cf, bf)
    m = cb[..., None] * decay
    y_diag = jnp.einsum('bclsgr,bcsgrp->bclgrp', m, xdt)
    decay_end = jnp.exp(a_cs[:, :, -1:] - a_cs)
    states = jnp.einsum('bclgn,bclgr,bclgrp->bcgrpn', bf, decay_end, xdt)
    chunk_decay = jnp.exp(a_cs[:, :, -1])

    def step(s, inp):
        st, dc = inp
        return s * dc[..., None, None] + st, s

    s0 = jnp.zeros((b, SSD_GROUPS, r, SSD_HEADDIM, SSD_STATE), jnp.float32)
    s_fin, s_prev = lax.scan(step, s0, (jnp.moveaxis(states, 1, 0), jnp.moveaxis(chunk_decay, 1, 0)))
    s_prev = jnp.moveaxis(s_prev, 0, 1)
    y_off = jnp.einsum('bclgn,bcgrpn,bclgr->bclgrp', cf, s_prev, jnp.exp(a_cs))
    y = (y_diag + y_off).reshape(b, L, SSD_HEADS, SSD_HEADDIM)
    return y, s_fin.reshape(b, SSD_HEADS, SSD_HEADDIM, SSD_STATE)


def ssd_recurrent(x, dt, a, bm, cm, s0):
    b = x.shape[0]
    r = SSD_HEADS // SSD_GROUPS
    ar = a.reshape(SSD_GROUPS, r)

    def step(s, inp):
        xt, dtt, bt, ct = inp
        xt = xt.astype(jnp.float32).reshape(b, SSD_GROUPS, r, SSD_HEADDIM)
        dtt = dtt.reshape(b, SSD_GROUPS, r)
        s = s * jnp.exp(dtt * ar)[..., None, None] + jnp.einsum('bgrp,bgn->bgrpn', xt * dtt[..., None], bt.astype(jnp.float32))
        yt = jnp.einsum('bgrpn,bgn->bgrp', s, ct.astype(jnp.float32))
        return s, yt.reshape(b, SSD_HEADS, SSD_HEADDIM)

    s_init = s0.astype(jnp.float32).reshape(b, SSD_GROUPS, r, SSD_HEADDIM, SSD_STATE)
    xs_t = (jnp.moveaxis(x, 1, 0), jnp.moveaxis(dt, 1, 0), jnp.moveaxis(bm, 1, 0), jnp.moveaxis(cm, 1, 0))
    s_fin, ys = lax.scan(step, s_init, xs_t)
    return jnp.moveaxis(ys, 0, 1), s_fin.reshape(b, SSD_HEADS, SSD_HEADDIM, SSD_STATE)


def _mixer_layer(u, ssd_buf, sc_buf, ssm_state, norm_w, w_in, conv_ssd_w, conv_ssd_b, dt_bias, a_log,
                 d_skip, ssd_norm_w, conv_sc_w, sc_norm_w, w_out):
    b, t = u.shape[0], u.shape[1]
    hn = rmsnorm(u, norm_w)
    z_ssd, xbc, dt_raw, z_sc, b_sc, c_sc, h_sc = jnp.split(hn @ w_in, SPLITS, axis=-1)
    xbc_full = jnp.concatenate([ssd_buf.astype(xbc.dtype), xbc], axis=1)
    xbc_c = jax.nn.silu(causal_dwconv(xbc_full, conv_ssd_w) + conv_ssd_b)
    new_ssd_buf = xbc_full[:, -(SSD_CONV - 1):]
    xs = xbc_c[..., :D_SSD].reshape(b, t, SSD_HEADS, SSD_HEADDIM)
    bm = xbc_c[..., D_SSD:D_SSD + D_GN].reshape(b, t, SSD_GROUPS, SSD_STATE)
    cm = xbc_c[..., D_SSD + D_GN:].reshape(b, t, SSD_GROUPS, SSD_STATE)
    dt = jax.nn.softplus(dt_raw.astype(jnp.float32) + dt_bias.astype(jnp.float32))
    a = -jnp.exp(a_log.astype(jnp.float32))
    if ssm_state is None:
        pad = CHUNK - N_META
        padf = lambda v: jnp.pad(v, [(0, 0), (pad, 0)] + [(0, 0)] * (v.ndim - 2))
        y, new_state = ssd_chunked(padf(xs), padf(dt), a, padf(bm), padf(cm))
        y = y[:, pad:]
    else:
        y, new_state = ssd_recurrent(xs, dt, a, bm, cm, ssm_state)
    y = y + d_skip.astype(jnp.float32)[:, None] * xs.astype(jnp.float32)
    y = y.reshape(b, t, D_SSD) * jax.nn.silu(z_ssd.astype(jnp.float32))
    y_ssd = rmsnorm(y, ssd_norm_w).astype(u.dtype)
    v = c_sc * h_sc
    v_full = jnp.concatenate([sc_buf.astype(v.dtype), v], axis=1)
    new_sc_buf = v_full[:, -(SC_CONV - 1):]
    y_sc = b_sc * causal_dwconv(v_full, conv_sc_w)
    y_sc = rmsnorm(y_sc * jax.nn.silu(z_sc), sc_norm_w)
    out = jnp.concatenate([y_ssd, y_sc], axis=-1) @ w_out
    return u + out, new_state, new_ssd_buf, new_sc_buf


def setup_inputs(seed: int = 0) -> dict:
    key = jax.random.key(seed)
    ks = jax.random.split(key, 20)
    f32 = jnp.float32
    x_prompt = jax.random.normal(ks[0], (BATCH, SEQ, D_MODEL), f32)
    x_sample = jax.random.normal(ks[1], (DEC_BATCH, DEC_SEQ, D_MODEL), f32)
    state_ssm = 0.1 * jax.random.normal(ks[2], (DEPTH, DEC_BATCH, SSD_HEADS, SSD_HEADDIM, SSD_STATE), f32)
    state_ssd_conv = jax.random.normal(ks[3], (DEPTH, DEC_BATCH, SSD_CONV - 1, D_XBC), f32)
    state_short_conv = jax.random.normal(ks[4], (DEPTH, DEC_BATCH, SC_CONV - 1, D_SC), f32)
    meta_tokens = jax.random.normal(ks[5], (N_META, D_MODEL), f32)
    norm_w = 1.0 + 0.02 * jax.random.normal(ks[6], (DEPTH, D_MODEL), f32)
    w_in = jax.random.normal(ks[7], (DEPTH, D_MODEL, D_IN_PROJ), f32) * D_MODEL ** -0.5
    conv_ssd_w = jax.random.normal(ks[8], (DEPTH, SSD_CONV, D_XBC), f32) * SSD_CONV ** -0.5
    conv_ssd_b = 0.02 * jax.random.normal(ks[9], (DEPTH, D_XBC), f32)
    dt0 = jnp.exp(jax.random.uniform(ks[10], (DEPTH, SSD_HEADS), f32) * (math.log(0.1) - math.log(0.001)) + math.log(0.001))
    dt_bias = dt0 + jnp.log(-jnp.expm1(-dt0))
    a_log = jnp.log(jax.random.uniform(ks[11], (DEPTH, SSD_HEADS), f32, 1.0, 16.0))
    d_skip = 1.0 + 0.1 * jax.random.normal(ks[12], (DEPTH, SSD_HEADS), f32)
    ssd_norm_w = 1.0 + 0.02 * jax.random.normal(ks[13], (DEPTH, D_SSD), f32)
    conv_sc_w = jax.random.normal(ks[14], (DEPTH, SC_CONV, D_SC), f32) * SC_CONV ** -0.5
    sc_norm_w = 1.0 + 0.02 * jax.random.normal(ks[15], (DEPTH, D_SC), f32)
    w_out = jax.random.normal(ks[16], (DEPTH, D_MIX, D_MODEL), f32) * D_MIX ** -0.5
    final_norm_w = 1.0 + 0.02 * jax.random.normal(ks[17], (D_MODEL,), f32)
    return {"x_prompt": x_prompt, "x_sample": x_sample, "state_ssm": state_ssm,
            "state_ssd_conv": state_ssd_conv, "state_short_conv": state_short_conv,
            "meta_tokens": meta_tokens, "norm_w": norm_w, "w_in": w_in, "conv_ssd_w": conv_ssd_w,
            "conv_ssd_b": conv_ssd_b, "dt_bias": dt_bias, "a_log": a_log, "d_skip": d_skip,
            "ssd_norm_w": ssd_norm_w, "conv_sc_w": conv_sc_w, "sc_norm_w": sc_norm_w,
            "w_out": w_out, "final_norm_w": final_norm_w}


def reference(x_prompt, x_sample, state_ssm, state_ssd_conv, state_short_conv, meta_tokens, norm_w, w_in,
              conv_ssd_w, conv_ssd_b, dt_bias, a_log, d_skip, ssd_norm_w, conv_sc_w, sc_norm_w, w_out,
              final_norm_w):
    b_p = x_prompt.shape[0]
    meta = jnp.broadcast_to(meta_tokens.astype(x_prompt.dtype)[None], (b_p, N_META, D_MODEL))
    u_p = jnp.concatenate([meta, x_prompt], axis=1)
    u_s = x_sample
    ssm_p, cssd_p, csc_p, ssm_s, cssd_s, csc_s = [], [], [], [], [], []
    for l in range(DEPTH):
        lw = (norm_w[l], w_in[l], conv_ssd_w[l], conv_ssd_b[l], dt_bias[l], a_log[l], d_skip[l],
              ssd_norm_w[l], conv_sc_w[l], sc_norm_w[l], w_out[l])
        u_p, s1, c1, k1 = _mixer_layer(u_p, jnp.zeros((b_p, SSD_CONV - 1, D_XBC), u_p.dtype),
                                       jnp.zeros((b_p, SC_CONV - 1, D_SC), u_p.dtype), None, *lw)
        u_s, s2, c2, k2 = _mixer_layer(u_s, state_ssd_conv[l], state_short_conv[l], state_ssm[l], *lw)
        ssm_p.append(s1); cssd_p.append(c1); csc_p.append(k1)
        ssm_s.append(s2); cssd_s.append(c2); csc_s.append(k2)
    y_prompt = rmsnorm(u_p, final_norm_w)[:, N_META:]
    y_sample = rmsnorm(u_s, final_norm_w)
    return (y_prompt, y_sample, jnp.stack(ssm_p), jnp.stack(cssd_p), jnp.stack(csc_p),
            jnp.stack(ssm_s), jnp.stack(cssd_s), jnp.stack(csc_s))
```

```python
import functools

import jax
import jax.numpy as jnp
from jax import lax
from jax.experimental import pallas as pl
from jax.experimental.pallas import tpu as pltpu

F32 = jnp.float32
BF16 = jnp.bfloat16
EPS = 1e-5

N_META = 16
HEADDIM = 64
GROUPS = 4
STATE = 128
CHUNK = 128
LANES = 128
TW = 256

VMEM_LIMIT_INPROJ = 48 * 1024 * 1024
VMEM_LIMIT_MIXER = 58 * 1024 * 1024
VMEM_LIMIT_SAMPLE = 48 * 1024 * 1024


def _silu(x):
    return x / (1.0 + jnp.exp(-x))


def _softplus(x):
    return jnp.maximum(x, 0.0) + jnp.log1p(jnp.exp(-jnp.abs(x)))


def _split_hi_lo(x):
    hi = x.astype(BF16)
    lo = (x - hi.astype(F32)).astype(BF16)
    return hi, lo


def _shift_rows(x, hist8, k, row8):
    r = pltpu.roll(x, k, 0)
    hk = pltpu.roll(hist8, k, 0)
    first = jnp.where(row8 < k, hk, r[0:8])
    return jnp.concatenate([first, r[8:]], axis=0)


def _inproj_body(x_ref, nw_ref, w_ref, wdt_ref, p_ref, dt_ref, hn_ref, *, tm):
    rb = 16

    @pl.when(pl.program_id(1) == 0)
    def _():
        nw = nw_ref[...]

        def body(r, carry):
            sl = pl.ds(pl.multiple_of(r * rb, rb), rb)
            x = x_ref[sl, :]
            ms = jnp.mean(x * x, axis=-1, keepdims=True)
            hn_ref[sl, :] = (x * lax.rsqrt(ms + EPS) * nw).astype(BF16)
            return carry

        lax.fori_loop(0, tm // rb, body, 0)
        dt_ref[...] = jnp.dot(hn_ref[...], wdt_ref[...], preferred_element_type=F32)

    p_ref[...] = jnp.dot(hn_ref[...], w_ref[...], preferred_element_type=F32).astype(BF16)


def _inproj(x, norm_w, w_main, w_dt, *, tm, tn, name):
    rows, d = x.shape
    ncols = w_main.shape[1]
    grid = (rows // tm, ncols // tn)
    return pl.pallas_call(
        functools.partial(_inproj_body, tm=tm),
        grid=grid,
        in_specs=[
            pl.BlockSpec((tm, d), lambda i, j: (i, 0)),
            pl.BlockSpec((1, d), lambda i, j: (0, 0)),
            pl.BlockSpec((d, tn), lambda i, j: (0, j)),
            pl.BlockSpec((d, LANES), lambda i, j: (0, 0)),
        ],
        out_specs=[
            pl.BlockSpec((tm, tn), lambda i, j: (i, j)),
            pl.BlockSpec((tm, LANES), lambda i, j: (i, 0)),
        ],
        out_shape=[
            jax.ShapeDtypeStruct((rows, ncols), BF16),
            jax.ShapeDtypeStruct((rows, LANES), F32),
        ],
        scratch_shapes=[pltpu.VMEM((tm, d), BF16)],
        compiler_params=pltpu.CompilerParams(
            dimension_semantics=("arbitrary", "arbitrary"),
            vmem_limit_bytes=VMEM_LIMIT_INPROJ),
        name=name,
    )(x, norm_w, w_main, w_dt)


def _mixer_body(zs_ref, zc_ref, b_ref, c_ref, h_ref, xr_ref, bcr_ref, dtr_ref, u_ref,
                s0_ref, xt0_ref, vt0_ref,
                cw_ref, cbias_ref, dtb_ref, alog_ref, dexp_ref, nssd_ref, scw_ref, nsc_ref,
                wout_ref, nfin_ref, e2_ref,
                y_ref, st_ref, stT_ref, xt_ref, vt_ref,
                state_s, xh_s, vh_s, xc_s, exp_s, cbm_s, bt_s, at_s, xdt_s, xd_s, ypre_s,
                ytmp_s, ymix_s,
                *, n_pad, d, heads):
    c = pl.program_id(1)
    nc = pl.num_programs(1)
    d_gn = GROUPS * STATE
    d_xbc = d + 2 * d_gn
    gw = d // GROUPS

    @pl.when(c == 0)
    def _init():
        state_s[...] = s0_ref[...]
        xh_s[...] = xt0_ref[...]
        vh_s[...] = vt0_ref[...]

    row8 = lax.broadcasted_iota(jnp.int32, (8, TW), 0)
    rowi = lax.broadcasted_iota(jnp.int32, (CHUNK, LANES), 0)
    coli = lax.broadcasted_iota(jnp.int32, (CHUNK, LANES), 1)

    for t in range(d_xbc // TW):
        c0 = t * TW
        cols = slice(c0, c0 + TW)
        if c0 < d:
            x = xr_ref[:, cols].astype(F32)
        else:
            x = bcr_ref[:, c0 - d:c0 - d + TW].astype(F32)
        hist = xh_s[:, cols]
        acc = x * cw_ref[3:4, cols] + cbias_ref[0:1, cols]
        for k in (1, 2, 3):
            acc = acc + _shift_rows(x, hist, k, row8) * cw_ref[3 - k:4 - k, cols]
        xc_s[:, cols] = _silu(acc)
        xh_s[:, cols] = x[CHUNK - 8:CHUNK, :]

    dtv = dtr_ref[...] + dtb_ref[...]
    dt = _softplus(dtv)
    if n_pad:
        dt = jnp.where(rowi >= n_pad, dt, 0.0)
    dta = dt * (-jnp.exp(alog_ref[...]))
    tri = (rowi >= coli).astype(F32)
    a_cs = jnp.dot(tri, dta, precision=lax.Precision.HIGHEST, preferred_element_type=F32)
    at_s[...] = a_cs.T
    stack = jnp.concatenate(
        [dt, jnp.exp(a_cs), jnp.exp(a_cs[CHUNK - 1:CHUNK, :] - a_cs)], axis=0)
    hi, lo = _split_hi_lo(stack)
    exp_s[...] = jnp.dot(jnp.concatenate([hi, lo], axis=1), e2_ref[...],
                         preferred_element_type=F32)

    for g in range(GROUPS):
        bg = xc_s[:, d + g * STATE:d + (g + 1) * STATE]
        cg = xc_s[:, d + d_gn + g * STATE:d + d_gn + (g + 1) * STATE]
        cbm_s[g] = lax.dot_general(cg.astype(BF16), bg.astype(BF16),
                                   (((1,), (1,)), ((), ())), preferred_element_type=F32)
        bt_s[g] = bg.T.astype(BF16)

    for t in range(d // TW):
        cols = slice(t * TW, (t + 1) * TW)
        xdt = xc_s[:, cols] * exp_s[0:CHUNK, cols]
        xdt_s[:, cols] = xdt.astype(BF16)
        xd_s[:, cols] = (xdt * exp_s[2 * CHUNK:3 * CHUNK, cols]).astype(BF16)

    cdec = exp_s[2 * CHUNK - 1:2 * CHUNK, :]
    for g in range(GROUPS):
        cols = slice(g * gw, (g + 1) * gw)
        s_prev = state_s[:, cols]
        cg = xc_s[:, d + d_gn + g * STATE:d + d_gn + (g + 1) * STATE].astype(BF16)
        yoff = jnp.dot(cg, s_prev.astype(BF16), preferred_element_type=F32)
        ypre_s[:, cols] = yoff * exp_s[CHUNK:2 * CHUNK, cols]
        state_s[:, cols] = s_prev * cdec[:, cols] + jnp.dot(
            bt_s[g], xd_s[:, cols], preferred_element_type=F32)

    causal = rowi >= coli
    lane_lo = coli < HEADDIM
    ss = jnp.zeros((CHUNK, LANES), F32)
    for pr in range(heads // 2):
        g = (2 * pr * HEADDIM) // gw
        ms = []
        for hh in (2 * pr, 2 * pr + 1):
            rowb = jnp.broadcast_to(at_s[hh:hh + 1, :], (CHUNK, CHUNK))
            seg = rowb.T - rowb
            decay = jnp.exp(jnp.where(causal, seg, -jnp.inf))
            ms.append((cbm_s[g] * decay).astype(BF16))
        lhs = jnp.concatenate(ms, axis=1)
        cols = slice(pr * LANES, (pr + 1) * LANES)
        xp = xdt_s[:, cols]
        zero = jnp.zeros_like(xp)
        rhs = jnp.concatenate([jnp.where(lane_lo, xp, zero), jnp.where(lane_lo, zero, xp)], axis=0)
        y = jnp.dot(lhs, rhs, preferred_element_type=F32)
        y = y + ypre_s[:, cols] + dexp_ref[0:1, cols] * xc_s[:, cols]
        y = y * _silu(zs_ref[:, cols].astype(F32))
        ytmp_s[:, cols] = y
        ss = ss + y * y
    rs = lax.rsqrt(jnp.sum(ss, axis=-1, keepdims=True) / d + EPS)
    for t in range(d // TW):
        cols = slice(t * TW, (t + 1) * TW)
        ymix_s[:, cols] = (ytmp_s[:, cols] * rs * nssd_ref[0:1, cols]).astype(BF16)

    ss = jnp.zeros((CHUNK, TW), F32)
    for t in range(d // TW):
        cols = slice(t * TW, (t + 1) * TW)
        v = c_ref[:, cols].astype(F32) * h_ref[:, cols].astype(F32)
        hist = vh_s[:, cols]
        acc = v * scw_ref[2:3, cols]
        for k in (1, 2):
            acc = acc + _shift_rows(v, hist, k, row8) * scw_ref[2 - k:3 - k, cols]
        ysc = b_ref[:, cols].astype(F32) * acc
        ysc = ysc * _silu(zc_ref[:, cols].astype(F32))
        ytmp_s[:, cols] = ysc
        vh_s[:, cols] = v[CHUNK - 8:CHUNK, :]
        ss = ss + ysc * ysc
    rs = lax.rsqrt(jnp.sum(ss, axis=-1, keepdims=True) / d + EPS)
    for t in range(d // TW):
        cols = slice(t * TW, (t + 1) * TW)
        ymix_s[:, d + t * TW:d + (t + 1) * TW] = (
            ytmp_s[:, cols] * rs * nsc_ref[0:1, cols]).astype(BF16)

    out = jnp.dot(ymix_s[...], wout_ref[...], preferred_element_type=F32)
    u = u_ref[...] + out
    ms = jnp.mean(u * u, axis=-1, keepdims=True)
    y_ref[...] = u * lax.rsqrt(ms + EPS) * nfin_ref[...]

    @pl.when(c == nc - 1)
    def _fin():
        stT_ref[0] = state_s[...]
        for t in range(d // LANES):
            st_ref[0, t * LANES:(t + 1) * LANES, :] = state_s[:, t * LANES:(t + 1) * LANES].T
        xt_ref[0] = xh_s[...]
        vt_ref[0] = vh_s[...]


def _const_spec(shape):
    nd = len(shape)
    return pl.BlockSpec(shape, lambda b, c: (0,) * nd)


def _mixer(p, dtr, u, s0, xt0, vt0, wts, *, nb, nc, row_block0, n_pad, name):
    d = u.shape[1]
    d_gn = GROUPS * STATE
    d_xbc = d + 2 * d_gn
    heads = d // HEADDIM
    (cw, cbias, dtb, alog, dexp, nssd, scw, nsc, wout, nfin, e2) = wts

    def rowmap(colblk):
        return lambda b, c: (row_block0 + b * nc + c, colblk)

    seg = lambda k: pl.BlockSpec((CHUNK, d), rowmap(k))
    in_specs = [
        seg(0), seg(1), seg(2), seg(3), seg(4), seg(5),
        pl.BlockSpec((CHUNK, 2 * d_gn), rowmap((6 * d) // (2 * d_gn))),
        pl.BlockSpec((CHUNK, LANES), rowmap(0)),
        pl.BlockSpec((CHUNK, d), rowmap(0)),
        _const_spec((STATE, d)), _const_spec((8, d_xbc)), _const_spec((8, d)),
        _const_spec(cw.shape), _const_spec(cbias.shape), _const_spec(dtb.shape),
        _const_spec(alog.shape), _const_spec(dexp.shape), _const_spec(nssd.shape),
        _const_spec(scw.shape), _const_spec(nsc.shape),
        pl.BlockSpec(wout.shape, lambda b, c: (0, 0), pipeline_mode=pl.Buffered(1)),
        _const_spec(nfin.shape),
        pl.BlockSpec(e2.shape, lambda b, c: (0, 0), pipeline_mode=pl.Buffered(1)),
    ]
    out_specs = [
        pl.BlockSpec((CHUNK, d), lambda b, c: (b * nc + c, 0)),
        pl.BlockSpec((1, d, STATE), lambda b, c: (b, 0, 0)),
        pl.BlockSpec((1, STATE, d), lambda b, c: (b, 0, 0)),
        pl.BlockSpec((1, 8, d_xbc), lambda b, c: (b, 0, 0)),
        pl.BlockSpec((1, 8, d), lambda b, c: (b, 0, 0)),
    ]
    out_shape = [
        jax.ShapeDtypeStruct((nb * nc * CHUNK, d), F32),
        jax.ShapeDtypeStruct((nb, d, STATE), F32),
        jax.ShapeDtypeStruct((nb, STATE, d), F32),
        jax.ShapeDtypeStruct((nb, 8, d_xbc), F32),
        jax.ShapeDtypeStruct((nb, 8, d), F32),
    ]
    scratch = [
        pltpu.VMEM((STATE, d), F32),
        pltpu.VMEM((8, d_xbc), F32),
        pltpu.VMEM((8, d), F32),
        pltpu.VMEM((CHUNK, d_xbc), F32),
        pltpu.VMEM((3 * CHUNK, d), F32),
        pltpu.VMEM((GROUPS, CHUNK, CHUNK), F32),
        pltpu.VMEM((GROUPS, STATE, CHUNK), BF16),
        pltpu.VMEM((LANES, CHUNK), F32),
        pltpu.VMEM((CHUNK, d), BF16),
        pltpu.VMEM((CHUNK, d), BF16),
        pltpu.VMEM((CHUNK, d), F32),
        pltpu.VMEM((CHUNK, d), F32),
        pltpu.VMEM((CHUNK, 2 * d), BF16),
    ]
    return pl.pallas_call(
        functools.partial(_mixer_body, n_pad=n_pad, d=d, heads=heads),
        grid=(nb, nc),
        in_specs=in_specs,
        out_specs=out_specs,
        out_shape=out_shape,
        scratch_shapes=scratch,
        compiler_params=pltpu.CompilerParams(
            dimension_semantics=("arbitrary", "arbitrary"),
            vmem_limit_bytes=VMEM_LIMIT_MIXER),
        name=name,
    )(p, p, p, p, p, p, p, dtr, u, s0, xt0, vt0, cw, cbias, dtb, alog, dexp, nssd, scw, nsc,
      wout, nfin, e2)


def _sample_front_body(zs_ref, zc_ref, b_ref, c_ref, h_ref, xr_ref, bcr_ref, dtr_ref,
                       cst_ref, sst_ref,
                       cw_ref, cbias_ref, dtb_ref, alog_ref, dexp_ref, scw_ref, nsc_ref, e2_ref,
                       cst_o, sst_o, xthl_o, bmat_o, cmat_o, dec_o, xsd_o, gate_o, ysc_o,
                       xc_s, ytmp_s, *, d):
    d_gn = GROUPS * STATE
    d_xbc = d + 2 * d_gn
    nbt = zs_ref.shape[0]

    for t in range(d_xbc // TW):
        c0 = t * TW
        cols = slice(c0, c0 + TW)
        if c0 < d:
            x = xr_ref[:, cols].astype(F32)
        else:
            x = bcr_ref[:, c0 - d:c0 - d + TW].astype(F32)
        x0 = cst_ref[:, c0:c0 + TW]
        x1 = cst_ref[:, d_xbc + c0:d_xbc + c0 + TW]
        x2 = cst_ref[:, 2 * d_xbc + c0:2 * d_xbc + c0 + TW]
        acc = (x0 * cw_ref[0:1, cols] + x1 * cw_ref[1:2, cols] + x2 * cw_ref[2:3, cols]
               + x * cw_ref[3:4, cols] + cbias_ref[0:1, cols])
        xc_s[:, cols] = _silu(acc)
        cst_o[:, c0:c0 + TW] = x1
        cst_o[:, d_xbc + c0:d_xbc + c0 + TW] = x2
        cst_o[:, 2 * d_xbc + c0:2 * d_xbc + c0 + TW] = x

    dt = _softplus(dtr_ref[...] + dtb_ref[...])
    dec_o[...] = jnp.exp(dt * (-jnp.exp(alog_ref[...])))
    hi, lo = _split_hi_lo(dt)
    dt_exp = jnp.dot(jnp.concatenate([hi, lo], axis=1), e2_ref[...], preferred_element_type=F32)

    bmat_o[...] = xc_s[:, d:d + d_gn].astype(BF16)
    cmat_o[...] = xc_s[:, d + d_gn:d + 2 * d_gn]

    for t in range(d // LANES):
        cols = slice(t * LANES, (t + 1) * LANES)
        xs = xc_s[:, cols]
        xsd_o[:, cols] = xs * dexp_ref[0:1, cols]
        gate_o[:, cols] = _silu(zs_ref[:, cols].astype(F32))
        xdt_t = (xs * dt_exp[:, cols]).T
        hi, lo = _split_hi_lo(xdt_t)
        xthl_o[cols, 0:nbt] = hi
        xthl_o[cols, nbt:2 * nbt] = lo

    ss = jnp.zeros((nbt, TW), F32)
    for t in range(d // TW):
        c0 = t * TW
        cols = slice(c0, c0 + TW)
        v = c_ref[:, cols].astype(F32) * h_ref[:, cols].astype(F32)
        v0 = sst_ref[:, c0:c0 + TW]
        v1 = sst_ref[:, d + c0:d + c0 + TW]
        acc = v0 * scw_ref[0:1, cols] + v1 * scw_ref[1:2, cols] + v * scw_ref[2:3, cols]
        ysc = b_ref[:, cols].astype(F32) * acc
        ysc = ysc * _silu(zc_ref[:, cols].astype(F32))
        ytmp_s[:, cols] = ysc
        sst_o[:, c0:c0 + TW] = v1
        sst_o[:, d + c0:d + c0 + TW] = v
        ss = ss + ysc * ysc
    rs = lax.rsqrt(jnp.sum(ss, axis=-1, keepdims=True) / d + EPS)
    for t in range(d // TW):
        cols = slice(t * TW, (t + 1) * TW)
        ysc_o[:, cols] = (ytmp_s[:, cols] * rs * nsc_ref[0:1, cols]).astype(BF16)


def _sample_front(p, dtr, cst, sst, wts, *, row_block, d, name):
    d_gn = GROUPS * STATE
    d_xbc = d + 2 * d_gn
    nbt = cst.shape[0]
    (cw, cbias, dtb, alog, dexp, scw, nsc, e2) = wts
    seg = lambda k: pl.BlockSpec((nbt, d), lambda i: (row_block, k))
    full = lambda a: pl.BlockSpec(a.shape, lambda i: (0,) * a.ndim)
    in_specs = [
        seg(0), seg(1), seg(2), seg(3), seg(4), seg(5),
        pl.BlockSpec((nbt, 2 * d_gn), lambda i: (row_block, (6 * d) // (2 * d_gn))),
        pl.BlockSpec((nbt, LANES), lambda i: (row_block, 0)),
        full(cst), full(sst),
        full(cw), full(cbias), full(dtb), full(alog), full(dexp), full(scw), full(nsc), full(e2),
    ]
    out_shape = [
        jax.ShapeDtypeStruct(cst.shape, F32),
        jax.ShapeDtypeStruct(sst.shape, F32),
        jax.ShapeDtypeStruct((d, 2 * nbt), BF16),
        jax.ShapeDtypeStruct((nbt, d_gn), BF16),
        jax.ShapeDtypeStruct((nbt, d_gn), F32),
        jax.ShapeDtypeStruct((nbt, LANES), F32),
        jax.ShapeDtypeStruct((nbt, d), F32),
        jax.ShapeDtypeStruct((nbt, d), F32),
        jax.ShapeDtypeStruct((nbt, d), BF16),
    ]
    out_specs = [pl.BlockSpec(s.shape, lambda i: (0,) * len(s.shape)) for s in out_shape]
    return pl.pallas_call(
        functools.partial(_sample_front_body, d=d),
        grid=(1,),
        in_specs=in_specs,
        out_specs=out_specs,
        out_shape=out_shape,
        scratch_shapes=[pltpu.VMEM((nbt, d_xbc), F32), pltpu.VMEM((nbt, d), F32)],
        compiler_params=pltpu.CompilerParams(vmem_limit_bytes=VMEM_LIMIT_SAMPLE),
        name=name,
    )(p, p, p, p, p, p, p, dtr, cst, sst, cw, cbias, dtb, alog, dexp, scw, nsc, e2)


def _sample_update_body(dec_ref, st_ref, xthl_ref, bmat_ref, cmat_ref, so_ref, yraw_ref, *, heads):
    b = pl.program_id(0)
    nbt = bmat_ref.shape[0]
    d = xthl_ref.shape[0]
    gw = d // GROUPS
    hpg = heads // GROUPS
    rb = 16

    @pl.when(b == 0)
    def _():
        yraw_ref[...] = jnp.zeros_like(yraw_ref)

    rowi = lax.broadcasted_iota(jnp.int32, (nbt, STATE), 0)
    mask_b = rowi == b
    b16 = pl.multiple_of((b // rb) * rb, rb)
    row16 = (lax.broadcasted_iota(jnp.int32, (rb, STATE), 0) + b16) == b
    for g in range(GROUPS):
        bm = bmat_ref[:, g * STATE:(g + 1) * STATE]
        r = jnp.where(mask_b, bm, jnp.zeros_like(bm))
        upd = jnp.dot(xthl_ref[g * gw:(g + 1) * gw, :], jnp.concatenate([r, r], axis=0),
                      preferred_element_type=F32)
        parts = []
        for hh in range(hpg):
            r0 = g * gw + hh * HEADDIM
            parts.append(st_ref[0, r0:r0 + HEADDIM, :] * dec_ref[b * heads + g * hpg + hh])
        s_new = jnp.concatenate(parts, axis=0) + upd
        so_ref[0, g * gw:(g + 1) * gw, :] = s_new
        c16 = jnp.where(row16, cmat_ref[pl.ds(b16, rb), g * STATE:(g + 1) * STATE], 0.0)
        yg = lax.dot_general(c16.astype(BF16), s_new.astype(BF16),
                             (((1,), (1,)), ((), ())), preferred_element_type=F32)
        yraw_ref[pl.ds(b16, rb), g * gw:(g + 1) * gw] += yg


def _sample_update(dec_flat, state, xthl, bmat, cmat, *, heads, name):
    nbt, d, n = state.shape
    return pl.pallas_call(
        functools.partial(_sample_update_body, heads=heads),
        grid=(nbt,),
        in_specs=[
            pl.BlockSpec(memory_space=pltpu.SMEM),
            pl.BlockSpec((1, d, n), lambda b: (b, 0, 0)),
            pl.BlockSpec(xthl.shape, lambda b: (0, 0)),
            pl.BlockSpec(bmat.shape, lambda b: (0, 0)),
            pl.BlockSpec(cmat.shape, lambda b: (0, 0)),
        ],
        out_specs=[
            pl.BlockSpec((1, d, n), lambda b: (b, 0, 0)),
            pl.BlockSpec((nbt, d), lambda b: (0, 0)),
        ],
        out_shape=[
            jax.ShapeDtypeStruct(state.shape, F32),
            jax.ShapeDtypeStruct((nbt, d), F32),
        ],
        compiler_params=pltpu.CompilerParams(
            dimension_semantics=("arbitrary",), vmem_limit_bytes=VMEM_LIMIT_SAMPLE),
        name=name,
    )(dec_flat, state, xthl, bmat, cmat)


def _sample_back_body(yraw_ref, xsd_ref, gate_ref, ysc_ref, u_ref, nssd_ref, wout_ref, nfin_ref,
                      y_ref, ymix_s, *, d):
    y = (yraw_ref[...] + xsd_ref[...]) * gate_ref[...]
    rs = lax.rsqrt(jnp.mean(y * y, axis=-1, keepdims=True) + EPS)
    ymix_s[:, 0:d] = (y * rs * nssd_ref[...]).astype(BF16)
    ymix_s[:, d:2 * d] = ysc_ref[...]
    out = jnp.dot(ymix_s[...], wout_ref[...], preferred_element_type=F32)
    u = u_ref[...] + out
    ms = jnp.mean(u * u, axis=-1, keepdims=True)
    y_ref[...] = u * lax.rsqrt(ms + EPS) * nfin_ref[...]


def _sample_back(yraw, xsd, gate, ysc, u, nssd, wout, nfin, *, name):
    nbt, d = u.shape
    args = (yraw, xsd, gate, ysc, u, nssd, wout, nfin)
    return pl.pallas_call(
        functools.partial(_sample_back_body, d=d),
        grid=(1,),
        in_specs=[pl.BlockSpec(a.shape, lambda i: (0,) * a.ndim) for a in args],
        out_specs=pl.BlockSpec((nbt, d), lambda i: (0, 0)),
        out_shape=jax.ShapeDtypeStruct((nbt, d), F32),
        scratch_shapes=[pltpu.VMEM((nbt, 2 * d), BF16)],
        compiler_params=pltpu.CompilerParams(vmem_limit_bytes=VMEM_LIMIT_SAMPLE),
        name=name,
    )(*args)


def kernel(x_prompt, x_sample, state_ssm, state_ssd_conv, state_short_conv, meta_tokens, norm_w, w_in,
           conv_ssd_w, conv_ssd_b, dt_bias, a_log, d_skip, ssd_norm_w, conv_sc_w, sc_norm_w, w_out,
           final_norm_w):
    nb, seq, d = x_prompt.shape
    nbt = x_sample.shape[0]
    depth = norm_w.shape[0]
    assert depth == 1 and x_sample.shape[1] == 1
    heads = d // HEADDIM
    d_gn = GROUPS * STATE
    d_xbc = d + 2 * d_gn
    n_meta = meta_tokens.shape[0]
    assert seq % CHUNK == 0 and nbt == CHUNK and n_meta <= CHUNK and heads <= LANES
    nc = seq // CHUNK

    wi = w_in[0]
    o_z, o_x, o_dt = 0, d, d + d_xbc
    o_zc = o_dt + heads
    o_b, o_c, o_h = o_zc + d, o_zc + 2 * d, o_zc + 3 * d
    w_main = jnp.concatenate(
        [wi[:, o_z:o_z + d], wi[:, o_zc:o_zc + d], wi[:, o_b:o_b + d], wi[:, o_c:o_c + d],
         wi[:, o_h:o_h + d], wi[:, o_x:o_x + d_xbc]], axis=1).astype(BF16)
    w_dt = jnp.pad(wi[:, o_dt:o_dt + heads], ((0, 0), (0, LANES - heads))).astype(BF16)
    wout = w_out[0].astype(BF16)
    nw = norm_w[0][None, :]
    cw = conv_ssd_w[0]
    cbias = conv_ssd_b[0][None, :]
    dtb = jnp.pad(dt_bias[0], (0, LANES - heads))[None, :]
    alog = jnp.pad(a_log[0], (0, LANES - heads))[None, :]
    dexp = jnp.repeat(d_skip[0], HEADDIM)[None, :]
    nssd = ssd_norm_w[0][None, :]
    scw = conv_sc_w[0]
    nsc = sc_norm_w[0][None, :]
    nfin = final_norm_w[None, :]
    k_i = lax.broadcasted_iota(jnp.int32, (2 * LANES, d), 0)
    c_i = lax.broadcasted_iota(jnp.int32, (2 * LANES, d), 1)
    e2 = ((k_i % LANES) == (c_i // HEADDIM)).astype(BF16)

    xp = x_prompt.reshape(nb * seq, d)
    tm = 1024 if (nb * seq) % 1024 == 0 else CHUNK
    p_main, dt_main = _inproj(xp, nw, w_main, w_dt, tm=tm, tn=1024, name="inproj_prompt")
    x_small = jnp.concatenate(
        [jnp.zeros((CHUNK - n_meta, d), F32), meta_tokens.astype(F32), x_sample[:, 0, :]], axis=0)
    p_small, dt_small = _inproj(x_small, nw, w_main, w_dt, tm=CHUNK + nbt, tn=1024,
                                name="inproj_small")

    mixer_wts = (cw, cbias, dtb, alog, dexp, nssd, scw, nsc, wout, nfin, e2)

    zeros_state = jnp.zeros((STATE, d), F32)
    _, _, st_t_meta, xt_meta, vt_meta = _mixer(
        p_small, dt_small, x_small, zeros_state, jnp.zeros((8, d_xbc), F32), jnp.zeros((8, d), F32),
        mixer_wts, nb=1, nc=1, row_block0=0, n_pad=CHUNK - n_meta, name="mixer_meta")

    y_p, st_p, _, xt_p, vt_p = _mixer(
        p_main, dt_main, xp, st_t_meta[0], xt_meta[0], vt_meta[0],
        mixer_wts, nb=nb, nc=nc, row_block0=0, n_pad=0, name="mixer_prompt")

    cst = state_ssd_conv[0].reshape(nbt, 3 * d_xbc)
    sst = state_short_conv[0].reshape(nbt, 2 * d)
    (cst_n, sst_n, xthl, bmat, cmat, dec, xsd, gate, ysc) = _sample_front(
        p_small, dt_small, cst, sst, (cw, cbias, dtb, alog, dexp, scw, nsc, e2),
        row_block=1, d=d, name="sample_front")
    st_s, yraw = _sample_update(
        dec[:, :heads].reshape(-1), state_ssm[0].reshape(nbt, d, STATE), xthl, bmat, cmat,
        heads=heads, name="sample_update")
    y_s = _sample_back(yraw, xsd, gate, ysc, x_sample[:, 0, :], nssd, wout, nfin, name="sample_back")

    return (
        y_p.reshape(nb, seq, d),
        y_s.reshape(nbt, 1, d),
        st_p.reshape(1, nb, heads, HEADDIM, STATE),
        xt_p[:, 5:8, :][None],
        vt_p[:, 6:8, :][None],
        st_s.reshape(1, nbt, heads, HEADDIM, STATE),
        cst_n.reshape(1, nbt, 3, d_xbc),
        sst_n.reshape(1, nbt, 2, d),
    )
```

```python
import functools

import jax
import jax.numpy as jnp
from jax import lax
from jax.experimental import pallas as pl
from jax.experimental.pallas import tpu as pltpu

F32 = jnp.float32
BF16 = jnp.bfloat16
U32 = jnp.uint32
EPS = 1e-5

HEADDIM = 64
GROUPS = 4
STATE = 128
CHUNK = 128
LANES = 128
TW = 256
TN = 1024
NORM_ROWS = 256

VMEM_LIMIT_INPROJ = 52 * 1024 * 1024
VMEM_LIMIT_MIXER = 58 * 1024 * 1024
VMEM_LIMIT_SMALL = 48 * 1024 * 1024


def _silu(x):
    h = 0.5 * x
    return h + h * jnp.tanh(h)


def _softplus(x):
    return jnp.maximum(x, 0.0) + jnp.log1p(jnp.exp(-jnp.abs(x)))


def _split_hi_lo(x):
    hi = x.astype(BF16)
    lo = (x - hi.astype(F32)).astype(BF16)
    return hi, lo


def _pack(x_bf16):
    return pltpu.bitcast(x_bf16, U32)


def _unpack(x_u32):
    return pltpu.bitcast(x_u32, BF16)


def _pack_rows_body(w_ref, o_ref):
    o_ref[...] = _pack(w_ref[...].astype(BF16))


def _pack_rows(w, *, rows_per_step, name):
    k, n = w.shape
    return pl.pallas_call(
        _pack_rows_body,
        grid=(k // rows_per_step,),
        in_specs=[pl.BlockSpec((rows_per_step, n), lambda i: (i, 0))],
        out_specs=pl.BlockSpec((rows_per_step // 2, n), lambda i: (i, 0)),
        out_shape=jax.ShapeDtypeStruct((k // 2, n), U32),
        compiler_params=pltpu.CompilerParams(dimension_semantics=("arbitrary",)),
        name=name,
    )(w)


def _norm_body(xp_ref, xs_ref, nw_ref, wdt_ref, hn_ref, dt_ref, wdt_s, *, n_prompt_steps, heads):
    i = pl.program_id(0)
    rb = 16

    @pl.when(i == 0)
    def _():
        lane = lax.broadcasted_iota(jnp.int32, wdt_ref.shape, 1)
        wdt_s[...] = jnp.where(lane < heads, wdt_ref[...], 0.0).astype(BF16)

    def norm_from(src_ref):
        nw = nw_ref[...]

        def body(r, carry):
            x = src_ref[pl.ds(pl.multiple_of(r * rb, rb), rb), :]
            ms = jnp.mean(x * x, axis=-1, keepdims=True)
            hn = (x * lax.rsqrt(ms + EPS) * nw).astype(BF16)
            hn_ref[pl.ds(pl.multiple_of(r * (rb // 2), rb // 2), rb // 2), :] = _pack(hn)
            return carry

        lax.fori_loop(0, NORM_ROWS // rb, body, 0)

    @pl.when(i < n_prompt_steps)
    def _():
        norm_from(xp_ref)

    @pl.when(i >= n_prompt_steps)
    def _():
        norm_from(xs_ref)

    dt_ref[...] = jnp.dot(_unpack(hn_ref[...]), wdt_s[...], preferred_element_type=F32)


def _norm(xp, xs, nw, w_raw, *, dt_col_block, heads, name):
    rows_p, d = xp.shape
    rows_s = xs.shape[0]
    assert rows_p % NORM_ROWS == 0 and rows_s == NORM_ROWS
    n_prompt_steps = rows_p // NORM_ROWS
    rows = rows_p + rows_s
    return pl.pallas_call(
        functools.partial(_norm_body, n_prompt_steps=n_prompt_steps, heads=heads),
        grid=(n_prompt_steps + 1,),
        in_specs=[
            pl.BlockSpec((NORM_ROWS, d), lambda i: (jnp.minimum(i, n_prompt_steps - 1), 0)),
            pl.BlockSpec((NORM_ROWS, d), lambda i: (0, 0)),
            pl.BlockSpec((1, d), lambda i: (0, 0)),
            pl.BlockSpec((d, LANES), lambda i: (0, dt_col_block)),
        ],
        out_specs=[
            pl.BlockSpec((NORM_ROWS // 2, d), lambda i: (i, 0)),
            pl.BlockSpec((NORM_ROWS, LANES), lambda i: (i, 0)),
        ],
        out_shape=[
            jax.ShapeDtypeStruct((rows // 2, d), U32),
            jax.ShapeDtypeStruct((rows, LANES), F32),
        ],
        scratch_shapes=[pltpu.VMEM((d, LANES), BF16)],
        compiler_params=pltpu.CompilerParams(dimension_semantics=("arbitrary",)),
        name=name,
    )(xp, xs, nw, w_raw)


def _inproj_body(hn_ref, wm_ref, wx_ref, p_ref, wbf_s, *, j_shift_lo, j_shift_hi, shift):
    j = pl.program_id(0)
    i = pl.program_id(1)
    k = wm_ref.shape[0]
    rg = 32
    shifted = jnp.logical_and(j >= j_shift_lo, j < j_shift_hi)

    @pl.when(jnp.logical_and(i == 0, jnp.logical_not(shifted)))
    def _():
        def body(r, carry):
            sl = pl.ds(pl.multiple_of(r * rg, rg), rg)
            wbf_s[sl, :] = wm_ref[sl, :].astype(BF16)
            return carry

        lax.fori_loop(0, k // rg, body, 0)

    @pl.when(jnp.logical_and(i == 0, shifted))
    def _():
        width = TN + LANES

        def body(r, carry):
            sl = pl.ds(pl.multiple_of(r * rg, rg), rg)
            w = jnp.concatenate([wm_ref[sl, :], wx_ref[sl, :]], axis=1)
            w = pltpu.roll(w, width - shift, 1)
            wbf_s[sl, :] = w[:, :TN].astype(BF16)
            return carry

        lax.fori_loop(0, k // rg, body, 0)

    acc = jnp.dot(_unpack(hn_ref[...]), wbf_s[...], preferred_element_type=F32)
    p_ref[...] = _pack(acc.astype(BF16))


def _inproj(hn32, w_raw, *, tm, col_blocks, extra_blocks, j_shift_lo, j_shift_hi, shift, name):
    rows2, d = hn32.shape
    nj = len(col_blocks)
    ni = (2 * rows2) // tm
    cb = jnp.asarray(col_blocks, jnp.int32)
    xb = jnp.asarray(extra_blocks, jnp.int32)

    def wmap(j, i, cb_ref, xb_ref):
        return (0, cb_ref[j])

    def xmap(j, i, cb_ref, xb_ref):
        return (0, xb_ref[j])

    grid_spec = pltpu.PrefetchScalarGridSpec(
        num_scalar_prefetch=2,
        grid=(nj, ni),
        in_specs=[
            pl.BlockSpec((tm // 2, d), lambda j, i, cb_ref, xb_ref: (i, 0)),
            pl.BlockSpec((d, TN), wmap),
            pl.BlockSpec((d, LANES), xmap),
        ],
        out_specs=pl.BlockSpec((tm // 2, TN), lambda j, i, cb_ref, xb_ref: (i, j)),
        scratch_shapes=[pltpu.VMEM((d, TN), BF16)],
    )

    def body(cb_ref, xb_ref, *refs):
        _inproj_body(*refs, j_shift_lo=j_shift_lo, j_shift_hi=j_shift_hi, shift=shift)

    return pl.pallas_call(
        body,
        grid_spec=grid_spec,
        out_shape=jax.ShapeDtypeStruct((rows2, nj * TN), U32),
        compiler_params=pltpu.CompilerParams(
            dimension_semantics=("arbitrary", "arbitrary"),
            vmem_limit_bytes=VMEM_LIMIT_INPROJ),
        name=name,
    )(cb, xb, hn32, w_raw, w_raw)


def _conv4(ext, w_ref, cols):
    s1 = pltpu.roll(ext, 1, 0)
    q = ext * w_ref[1:2, cols] + s1 * w_ref[0:1, cols]
    out = ext * w_ref[3:4, cols] + s1 * w_ref[2:3, cols] + pltpu.roll(q, 2, 0)
    return out[8:, :]


def _conv3(ext, w_ref, cols):
    s1 = pltpu.roll(ext, 1, 0)
    q = ext * w_ref[1:2, cols] + s1 * w_ref[0:1, cols]
    out = ext * w_ref[2:3, cols] + pltpu.roll(q, 1, 0)
    return out[8:, :]


def _mixer_body(zs_ref, zc_ref, b_ref, c_ref, h_ref, xr_ref, bcr_ref, dtr_ref, u_ref,
                s0_ref, xt0_ref, vt0_ref,
                cw_ref, cbias_ref, dtb_ref, alog_ref, dexp_ref, nssd_ref, scw_ref, nsc_ref,
                wout_ref, nfin_ref, e2_ref,
                y_ref, st_ref, stT_ref, xt_ref, vt_ref,
                state_s, xh_s, vh_s, xc_s, exp_s, cbm_s, bt_s, at_s, xdt_s, xd_s, ypre_s,
                ytmp_s, ymix_s, ymixp_s,
                *, n_pad, d, heads, nc, total):
    s = pl.program_id(0)
    c = s % nc
    d_gn = GROUPS * STATE
    d_xbc = d + 2 * d_gn
    gw = d // GROUPS

    @pl.when(s == 0)
    def _zero():
        ymix_s[...] = jnp.zeros_like(ymix_s)

    @pl.when(c == 0)
    def _init():
        state_s[...] = s0_ref[...]
        xh_s[...] = xt0_ref[...]
        vh_s[...] = vt0_ref[...]

    ymixp_s[...] = ymix_s[...]
    out = jnp.dot(ymixp_s[...], _unpack(wout_ref[...]), preferred_element_type=F32)
    u = u_ref[...] + out
    ms = jnp.mean(u * u, axis=-1, keepdims=True)
    y_ref[...] = u * lax.rsqrt(ms + EPS) * nfin_ref[...]

    rowi = lax.broadcasted_iota(jnp.int32, (CHUNK, LANES), 0)
    coli = lax.broadcasted_iota(jnp.int32, (CHUNK, LANES), 1)

    for t in range(d_xbc // TW):
        c0 = t * TW
        cols = slice(c0, c0 + TW)
        if c0 < d:
            x = _unpack(xr_ref[:, cols]).astype(F32)
        else:
            x = _unpack(bcr_ref[:, c0 - d:c0 - d + TW]).astype(F32)
        ext = jnp.concatenate([xh_s[:, cols], x], axis=0)
        xc_s[:, cols] = _silu(_conv4(ext, cw_ref, cols) + cbias_ref[0:1, cols])
        xh_s[:, cols] = x[CHUNK - 8:CHUNK, :]

    dtv = dtr_ref[...] + dtb_ref[...]
    dt = _softplus(dtv)
    if n_pad:
        dt = jnp.where(rowi >= n_pad, dt, 0.0)
    dta = dt * (-jnp.exp(alog_ref[...]))
    tri = (rowi >= coli).astype(F32)
    a_cs = jnp.dot(tri, dta, precision=lax.Precision.HIGHEST, preferred_element_type=F32)
    at_s[...] = a_cs.T
    stack = jnp.concatenate(
        [dt, jnp.exp(a_cs), jnp.exp(a_cs[CHUNK - 1:CHUNK, :] - a_cs)], axis=0)
    hi, lo = _split_hi_lo(stack)
    exp_s[...] = jnp.dot(jnp.concatenate([hi, lo], axis=1), e2_ref[...],
                         preferred_element_type=F32)

    for g in range(GROUPS):
        bg = xc_s[:, d + g * STATE:d + (g + 1) * STATE]
        cg = xc_s[:, d + d_gn + g * STATE:d + d_gn + (g + 1) * STATE]
        cbm_s[g] = lax.dot_general(cg.astype(BF16), bg.astype(BF16),
                                   (((1,), (1,)), ((), ())), preferred_element_type=F32)
        bt_s[g] = bg.T.astype(BF16)

    for t in range(d // TW):
        cols = slice(t * TW, (t + 1) * TW)
        xdt = xc_s[:, cols] * exp_s[0:CHUNK, cols]
        xdt_s[:, cols] = xdt.astype(BF16)
        xd_s[:, cols] = (xdt * exp_s[2 * CHUNK:3 * CHUNK, cols]).astype(BF16)

    cdec = exp_s[2 * CHUNK - 1:2 * CHUNK, :]
    for g in range(GROUPS):
        cols = slice(g * gw, (g + 1) * gw)
        s_prev = state_s[:, cols]
        cg = xc_s[:, d + d_gn + g * STATE:d + d_gn + (g + 1) * STATE].astype(BF16)
        yoff = jnp.dot(cg, s_prev.astype(BF16), preferred_element_type=F32)
        ypre_s[:, cols] = yoff * exp_s[CHUNK:2 * CHUNK, cols]
        state_s[:, cols] = s_prev * cdec[:, cols] + jnp.dot(
            bt_s[g], xd_s[:, cols], preferred_element_type=F32)

    causal = rowi >= coli
    lane_lo = coli < HEADDIM
    ss = jnp.zeros((CHUNK, LANES), F32)
    for pr in range(heads // 2):
        g = (2 * pr * HEADDIM) // gw
        ms_ = []
        for hh in (2 * pr, 2 * pr + 1):
            rowb = jnp.broadcast_to(at_s[hh:hh + 1, :], (CHUNK, CHUNK))
            seg = rowb.T - rowb
            decay = jnp.exp(jnp.where(causal, seg, -jnp.inf))
            ms_.append((cbm_s[g] * decay).astype(BF16))
        lhs = jnp.concatenate(ms_, axis=1)
        cols = slice(pr * LANES, (pr + 1) * LANES)
        xp = xdt_s[:, cols]
        zero = jnp.zeros_like(xp)
        rhs = jnp.concatenate([jnp.where(lane_lo, xp, zero), jnp.where(lane_lo, zero, xp)], axis=0)
        y = jnp.dot(lhs, rhs, preferred_element_type=F32)
        y = y + ypre_s[:, cols] + dexp_ref[0:1, cols] * xc_s[:, cols]
        y = y * _silu(_unpack(zs_ref[:, cols]).astype(F32))
        ytmp_s[:, cols] = y
        ss = ss + y * y
    rs = lax.rsqrt(jnp.sum(ss, axis=-1, keepdims=True) / d + EPS)
    for t in range(d // TW):
        cols = slice(t * TW, (t + 1) * TW)
        ymix_s[:, cols] = (ytmp_s[:, cols] * rs * nssd_ref[0:1, cols]).astype(BF16)

    ss = jnp.zeros((CHUNK, TW), F32)
    for t in range(d // TW):
        cols = slice(t * TW, (t + 1) * TW)
        v = _unpack(c_ref[:, cols]).astype(F32) * _unpack(h_ref[:, cols]).astype(F32)
        ext = jnp.concatenate([vh_s[:, cols], v], axis=0)
        ysc = _unpack(b_ref[:, cols]).astype(F32) * _conv3(ext, scw_ref, cols)
        ysc = ysc * _silu(_unpack(zc_ref[:, cols]).astype(F32))
        ytmp_s[:, cols] = ysc
        vh_s[:, cols] = v[CHUNK - 8:CHUNK, :]
        ss = ss + ysc * ysc
    rs = lax.rsqrt(jnp.sum(ss, axis=-1, keepdims=True) / d + EPS)
    for t in range(d // TW):
        cols = slice(t * TW, (t + 1) * TW)
        ymix_s[:, d + t * TW:d + (t + 1) * TW] = (
            ytmp_s[:, cols] * rs * nsc_ref[0:1, cols]).astype(BF16)

    @pl.when(jnp.logical_and(c == nc - 1, s < total))
    def _fin():
        stT_ref[0] = state_s[...]
        for t in range(d // LANES):
            st_ref[0, t * LANES:(t + 1) * LANES, :] = state_s[:, t * LANES:(t + 1) * LANES].T
        xt_ref[0] = xh_s[...]
        vt_ref[0] = vh_s[...]


def _const_spec(shape):
    nd = len(shape)
    return pl.BlockSpec(shape, lambda s: (0,) * nd)


def _mixer(p32, dtr, u, s0, xt0, vt0, wts, *, nb, nc, row_block0, n_pad, name):
    d = u.shape[1]
    d_gn = GROUPS * STATE
    d_xbc = d + 2 * d_gn
    heads = d // HEADDIM
    total = nb * nc
    (cw, cbias, dtb, alog, dexp, nssd, scw, nsc, wout32, nfin, e2) = wts

    def rowmap(colblk):
        return lambda s: (row_block0 + jnp.minimum(s, total - 1), colblk)

    prev = lambda s: (jnp.maximum(s - 1, 0), 0)
    bmap = lambda s: (jnp.minimum(s // nc, nb - 1), 0, 0)

    seg = lambda k: pl.BlockSpec((CHUNK // 2, d), rowmap(k))
    in_specs = [
        seg(0), seg(1), seg(2), seg(3), seg(4), seg(5),
        pl.BlockSpec((CHUNK // 2, 2 * d_gn), rowmap((6 * d) // (2 * d_gn))),
        pl.BlockSpec((CHUNK, LANES), rowmap(0)),
        pl.BlockSpec((CHUNK, d), prev),
        _const_spec((STATE, d)), _const_spec((8, d_xbc)), _const_spec((8, d)),
        _const_spec(cw.shape), _const_spec(cbias.shape), _const_spec(dtb.shape),
        _const_spec(alog.shape), _const_spec(dexp.shape), _const_spec(nssd.shape),
        _const_spec(scw.shape), _const_spec(nsc.shape),
        pl.BlockSpec(wout32.shape, lambda s: (0, 0), pipeline_mode=pl.Buffered(1)),
        _const_spec(nfin.shape),
        pl.BlockSpec(e2.shape, lambda s: (0, 0), pipeline_mode=pl.Buffered(1)),
    ]
    out_specs = [
        pl.BlockSpec((CHUNK, d), prev),
        pl.BlockSpec((1, d, STATE), bmap),
        pl.BlockSpec((1, STATE, d), bmap),
        pl.BlockSpec((1, 8, d_xbc), bmap),
        pl.BlockSpec((1, 8, d), bmap),
    ]
    out_shape = [
        jax.ShapeDtypeStruct((total * CHUNK, d), F32),
        jax.ShapeDtypeStruct((nb, d, STATE), F32),
        jax.ShapeDtypeStruct((nb, STATE, d), F32),
        jax.ShapeDtypeStruct((nb, 8, d_xbc), F32),
        jax.ShapeDtypeStruct((nb, 8, d), F32),
    ]
    scratch = [
        pltpu.VMEM((STATE, d), F32),
        pltpu.VMEM((8, d_xbc), F32),
        pltpu.VMEM((8, d), F32),
        pltpu.VMEM((CHUNK, d_xbc), F32),
        pltpu.VMEM((3 * CHUNK, d), F32),
        pltpu.VMEM((GROUPS, CHUNK, CHUNK), F32),
        pltpu.VMEM((GROUPS, STATE, CHUNK), BF16),
        pltpu.VMEM((LANES, CHUNK), F32),
        pltpu.VMEM((CHUNK, d), BF16),
        pltpu.VMEM((CHUNK, d), BF16),
        pltpu.VMEM((CHUNK, d), F32),
        pltpu.VMEM((CHUNK, d), F32),
        pltpu.VMEM((CHUNK, 2 * d), BF16),
        pltpu.VMEM((CHUNK, 2 * d), BF16),
    ]
    return pl.pallas_call(
        functools.partial(_mixer_body, n_pad=n_pad, d=d, heads=heads, nc=nc, total=total),
        grid=(total + 1,),
        in_specs=in_specs,
        out_specs=out_specs,
        out_shape=out_shape,
        scratch_shapes=scratch,
        compiler_params=pltpu.CompilerParams(
            dimension_semantics=("arbitrary",),
            vmem_limit_bytes=VMEM_LIMIT_MIXER),
        name=name,
    )(p32, p32, p32, p32, p32, p32, p32, dtr, u, s0, xt0, vt0, cw, cbias, dtb, alog, dexp, nssd,
      scw, nsc, wout32, nfin, e2)


def _sample_front_body(zs_ref, zc_ref, b_ref, c_ref, h_ref, xr_ref, bcr_ref, dtr_ref,
                       cst_ref, sst_ref,
                       cw_ref, cbias_ref, dtb_ref, alog_ref, dexp_ref, scw_ref, nsc_ref, e2_ref,
                       cst_o, sst_o, xthl_o, bmat_o, cmat_o, dec_o, xsd_o, gate_o, ysc_o,
                       xc_s, ytmp_s, *, d):
    d_gn = GROUPS * STATE
    d_xbc = d + 2 * d_gn
    nbt = cst_ref.shape[0]

    for t in range(d_xbc // TW):
        c0 = t * TW
        cols = slice(c0, c0 + TW)
        if c0 < d:
            x = _unpack(xr_ref[:, cols]).astype(F32)
        else:
            x = _unpack(bcr_ref[:, c0 - d:c0 - d + TW]).astype(F32)
        x0 = cst_ref[:, c0:c0 + TW]
        x1 = cst_ref[:, d_xbc + c0:d_xbc + c0 + TW]
        x2 = cst_ref[:, 2 * d_xbc + c0:2 * d_xbc + c0 + TW]
        acc = (x0 * cw_ref[0:1, cols] + x1 * cw_ref[1:2, cols] + x2 * cw_ref[2:3, cols]
               + x * cw_ref[3:4, cols] + cbias_ref[0:1, cols])
        xc_s[:, cols] = _silu(acc)
        cst_o[:, c0:c0 + TW] = x1
        cst_o[:, d_xbc + c0:d_xbc + c0 + TW] = x2
        cst_o[:, 2 * d_xbc + c0:2 * d_xbc + c0 + TW] = x

    dt = _softplus(dtr_ref[...] + dtb_ref[...])
    dec_o[...] = jnp.exp(dt * (-jnp.exp(alog_ref[...])))
    hi, lo = _split_hi_lo(dt)
    dt_exp = jnp.dot(jnp.concatenate([hi, lo], axis=1), e2_ref[...], preferred_element_type=F32)

    bmat_o[...] = xc_s[:, d:d + d_gn].astype(BF16)
    cmat_o[...] = xc_s[:, d + d_gn:d + 2 * d_gn]

    for t in range(d // LANES):
        cols = slice(t * LANES, (t + 1) * LANES)
        xs = xc_s[:, cols]
        xsd_o[:, cols] = xs * dexp_ref[0:1, cols]
        gate_o[:, cols] = _silu(_unpack(zs_ref[:, cols]).astype(F32))
        xdt_t = (xs * dt_exp[:, cols]).T
        hi, lo = _split_hi_lo(xdt_t)
        xthl_o[cols, 0:nbt] = hi
        xthl_o[cols, nbt:2 * nbt] = lo

    ss = jnp.zeros((nbt, TW), F32)
    for t in range(d // TW):
        c0 = t * TW
        cols = slice(c0, c0 + TW)
        v = _unpack(c_ref[:, cols]).astype(F32) * _unpack(h_ref[:, cols]).astype(F32)
        v0 = sst_ref[:, c0:c0 + TW]
        v1 = sst_ref[:, d + c0:d + c0 + TW]
        acc = v0 * scw_ref[0:1, cols] + v1 * scw_ref[1:2, cols] + v * scw_ref[2:3, cols]
        ysc = _unpack(b_ref[:, cols]).astype(F32) * acc
        ysc = ysc * _silu(_unpack(zc_ref[:, cols]).astype(F32))
        ytmp_s[:, cols] = ysc
        sst_o[:, c0:c0 + TW] = v1
        sst_o[:, d + c0:d + c0 + TW] = v
        ss = ss + ysc * ysc
    rs = lax.rsqrt(jnp.sum(ss, axis=-1, keepdims=True) / d + EPS)
    for t in range(d // TW):
        cols = slice(t * TW, (t + 1) * TW)
        ysc_o[:, cols] = (ytmp_s[:, cols] * rs * nsc_ref[0:1, cols]).astype(BF16)


def _sample_front(p32, dtr, cst, sst, wts, *, row_block, d, name):
    d_gn = GROUPS * STATE
    d_xbc = d + 2 * d_gn
    nbt = cst.shape[0]
    (cw, cbias, dtb, alog, dexp, scw, nsc, e2) = wts
    seg = lambda k: pl.BlockSpec((nbt // 2, d), lambda i: (row_block, k))
    full = lambda a: pl.BlockSpec(a.shape, lambda i: (0,) * a.ndim)
    in_specs = [
        seg(0), seg(1), seg(2), seg(3), seg(4), seg(5),
        pl.BlockSpec((nbt // 2, 2 * d_gn), lambda i: (row_block, (6 * d) // (2 * d_gn))),
        pl.BlockSpec((nbt, LANES), lambda i: (row_block, 0)),
        full(cst), full(sst),
        full(cw), full(cbias), full(dtb), full(alog), full(dexp), full(scw), full(nsc), full(e2),
    ]
    out_shape = [
        jax.ShapeDtypeStruct(cst.shape, F32),
        jax.ShapeDtypeStruct(sst.shape, F32),
        jax.ShapeDtypeStruct((d, 2 * nbt), BF16),
        jax.ShapeDtypeStruct((nbt, d_gn), BF16),
        jax.ShapeDtypeStruct((nbt, d_gn), F32),
        jax.ShapeDtypeStruct((nbt, LANES), F32),
        jax.ShapeDtypeStruct((nbt, d), F32),
        jax.ShapeDtypeStruct((nbt, d), F32),
        jax.ShapeDtypeStruct((nbt, d), BF16),
    ]
    out_specs = [pl.BlockSpec(s.shape, lambda i: (0,) * len(s.shape)) for s in out_shape]
    return pl.pallas_call(
        functools.partial(_sample_front_body, d=d),
        grid=(1,),
        in_specs=in_specs,
        out_specs=out_specs,
        out_shape=out_shape,
        scratch_shapes=[pltpu.VMEM((nbt, d_xbc), F32), pltpu.VMEM((nbt, d), F32)],
        compiler_params=pltpu.CompilerParams(vmem_limit_bytes=VMEM_LIMIT_SMALL),
        name=name,
    )(p32, p32, p32, p32, p32, p32, p32, dtr, cst, sst, cw, cbias, dtb, alog, dexp, scw, nsc, e2)


def _sample_update_body(dec_ref, st_ref, xthl_ref, bmat_ref, cmat_ref, so_ref, yraw_ref, *, heads):
    b = pl.program_id(0)
    nbt = bmat_ref.shape[0]
    d = xthl_ref.shape[0]
    gw = d // GROUPS
    hpg = heads // GROUPS
    rb = 16

    @pl.when(b == 0)
    def _():
        yraw_ref[...] = jnp.zeros_like(yraw_ref)

    rowi = lax.broadcasted_iota(jnp.int32, (nbt, STATE), 0)
    mask_b = rowi == b
    b16 = pl.multiple_of((b // rb) * rb, rb)
    row16 = (lax.broadcasted_iota(jnp.int32, (rb, STATE), 0) + b16) == b
    for g in range(GROUPS):
        bm = bmat_ref[:, g * STATE:(g + 1) * STATE]
        r = jnp.where(mask_b, bm, jnp.zeros_like(bm))
        upd = jnp.dot(xthl_ref[g * gw:(g + 1) * gw, :], jnp.concatenate([r, r], axis=0),
                      preferred_element_type=F32)
        parts = []
        for hh in range(hpg):
            r0 = g * gw + hh * HEADDIM
            parts.append(st_ref[0, r0:r0 + HEADDIM, :] * dec_ref[b * heads + g * hpg + hh])
        s_new = jnp.concatenate(parts, axis=0) + upd
        so_ref[0, g * gw:(g + 1) * gw, :] = s_new
        c16 = jnp.where(row16, cmat_ref[pl.ds(b16, rb), g * STATE:(g + 1) * STATE], 0.0)
        yg = lax.dot_general(c16.astype(BF16), s_new.astype(BF16),
                             (((1,), (1,)), ((), ())), preferred_element_type=F32)
        yraw_ref[pl.ds(b16, rb), g * gw:(g + 1) * gw] += yg


def _sample_update(dec_flat, state, xthl, bmat, cmat, *, heads, name):
    nbt, d, n = state.shape
    return pl.pallas_call(
        functools.partial(_sample_update_body, heads=heads),
        grid=(nbt,),
        in_specs=[
            pl.BlockSpec(memory_space=pltpu.SMEM),
            pl.BlockSpec((1, d, n), lambda b: (b, 0, 0)),
            pl.BlockSpec(xthl.shape, lambda b: (0, 0)),
            pl.BlockSpec(bmat.shape, lambda b: (0, 0)),
            pl.BlockSpec(cmat.shape, lambda b: (0, 0)),
        ],
        out_specs=[
            pl.BlockSpec((1, d, n), lambda b: (b, 0, 0)),
            pl.BlockSpec((nbt, d), lambda b: (0, 0)),
        ],
        out_shape=[
            jax.ShapeDtypeStruct(state.shape, F32),
            jax.ShapeDtypeStruct((nbt, d), F32),
        ],
        compiler_params=pltpu.CompilerParams(
            dimension_semantics=("arbitrary",), vmem_limit_bytes=VMEM_LIMIT_SMALL),
        name=name,
    )(dec_flat, state, xthl, bmat, cmat)


def _sample_back_body(yraw_ref, xsd_ref, gate_ref, ysc_ref, u_ref, nssd_ref, wout_ref, nfin_ref,
                      y_ref, ymix_s, *, d):
    y = (yraw_ref[...] + xsd_ref[...]) * gate_ref[...]
    rs = lax.rsqrt(jnp.mean(y * y, axis=-1, keepdims=True) + EPS)
    ymix_s[:, 0:d] = (y * rs * nssd_ref[...]).astype(BF16)
    ymix_s[:, d:2 * d] = ysc_ref[...]
    out = jnp.dot(ymix_s[...], _unpack(wout_ref[...]), preferred_element_type=F32)
    u = u_ref[...] + out
    ms = jnp.mean(u * u, axis=-1, keepdims=True)
    y_ref[...] = u * lax.rsqrt(ms + EPS) * nfin_ref[...]


def _sample_back(yraw, xsd, gate, ysc, u, nssd, wout32, nfin, *, name):
    nbt, d = u.shape
    args = (yraw, xsd, gate, ysc, u, nssd, wout32, nfin)
    return pl.pallas_call(
        functools.partial(_sample_back_body, d=d),
        grid=(1,),
        in_specs=[pl.BlockSpec(a.shape, lambda i: (0,) * a.ndim) for a in args],
        out_specs=pl.BlockSpec((nbt, d), lambda i: (0, 0)),
        out_shape=jax.ShapeDtypeStruct((nbt, d), F32),
        scratch_shapes=[pltpu.VMEM((nbt, 2 * d), BF16)],
        compiler_params=pltpu.CompilerParams(vmem_limit_bytes=VMEM_LIMIT_SMALL),
        name=name,
    )(*args)


def kernel(x_prompt, x_sample, state_ssm, state_ssd_conv, state_short_conv, meta_tokens, norm_w, w_in,
           conv_ssd_w, conv_ssd_b, dt_bias, a_log, d_skip, ssd_norm_w, conv_sc_w, sc_norm_w, w_out,
           final_norm_w):
    nb, seq, d = x_prompt.shape
    nbt = x_sample.shape[0]
    depth = norm_w.shape[0]
    assert depth == 1 and x_sample.shape[1] == 1
    heads = d // HEADDIM
    d_gn = GROUPS * STATE
    d_xbc = d + 2 * d_gn
    n_meta = meta_tokens.shape[0]
    assert seq % CHUNK == 0 and nbt == CHUNK and n_meta <= CHUNK and heads <= LANES
    nc = seq // CHUNK
    rows_p = nb * seq
    rows = rows_p + CHUNK + nbt

    o_dt = d + d_xbc
    o_sc = o_dt + heads
    assert d % TN == 0 and d_xbc % TN == 0 and (2 * d_gn) == TN and o_dt % TN == 0 and heads < LANES
    tiles_per_d = d // TN
    col_blocks = (list(range(tiles_per_d))
                  + [o_dt // TN + t for t in range(4 * tiles_per_d)]
                  + [d // TN + t for t in range(d_xbc // TN)])
    n_sc = 4 * tiles_per_d
    extra_blocks = [o_dt // LANES] * len(col_blocks)
    for t in range(n_sc):
        extra_blocks[tiles_per_d + t] = (o_dt + (t + 1) * TN) // LANES

    wi = w_in[0]
    nw = norm_w[0][None, :]
    cw = conv_ssd_w[0]
    cbias = conv_ssd_b[0][None, :]
    dtb = jnp.pad(dt_bias[0], (0, LANES - heads))[None, :]
    alog = jnp.pad(a_log[0], (0, LANES - heads))[None, :]
    dexp = jnp.repeat(d_skip[0], HEADDIM)[None, :]
    nssd = ssd_norm_w[0][None, :]
    scw = conv_sc_w[0]
    nsc = sc_norm_w[0][None, :]
    nfin = final_norm_w[None, :]
    k_i = lax.broadcasted_iota(jnp.int32, (2 * LANES, d), 0)
    c_i = lax.broadcasted_iota(jnp.int32, (2 * LANES, d), 1)
    e2 = ((k_i % LANES) == (c_i // HEADDIM)).astype(BF16)

    wout32 = _pack_rows(w_out[0], rows_per_step=512, name="pack_wout")

    xp = x_prompt.reshape(rows_p, d)
    x_small = jnp.concatenate(
        [jnp.zeros((CHUNK - n_meta, d), F32), meta_tokens.astype(F32), x_sample[:, 0, :]], axis=0)
    hn32, dt_raw = _norm(xp, x_small, nw, wi, dt_col_block=o_dt // LANES, heads=heads, name="norm")
    tm = rows // 8
    assert rows % 8 == 0 and tm % 16 == 0
    p32 = _inproj(hn32, wi, tm=tm, col_blocks=col_blocks, extra_blocks=extra_blocks,
                  j_shift_lo=tiles_per_d, j_shift_hi=tiles_per_d + n_sc, shift=heads, name="inproj")

    mixer_wts = (cw, cbias, dtb, alog, dexp, nssd, scw, nsc, wout32, nfin, e2)
    meta_block = rows_p // CHUNK
    sample_block = meta_block + 1

    zeros_state = jnp.zeros((STATE, d), F32)
    _, _, st_t_meta, xt_meta, vt_meta = _mixer(
        p32, dt_raw, x_small, zeros_state, jnp.zeros((8, d_xbc), F32), jnp.zeros((8, d), F32),
        mixer_wts, nb=1, nc=1, row_block0=meta_block, n_pad=CHUNK - n_meta, name="mixer_meta")

    y_p, st_p, _, xt_p, vt_p = _mixer(
        p32, dt_raw, xp, st_t_meta[0], xt_meta[0], vt_meta[0],
        mixer_wts, nb=nb, nc=nc, row_block0=0, n_pad=0, name="mixer_prompt")

    cst = state_ssd_conv[0].reshape(nbt, 3 * d_xbc)
    sst = state_short_conv[0].reshape(nbt, 2 * d)
    (cst_n, sst_n, xthl, bmat, cmat, dec, xsd, gate, ysc) = _sample_front(
        p32, dt_raw, cst, sst, (cw, cbias, dtb, alog, dexp, scw, nsc, e2),
        row_block=sample_block, d=d, name="sample_front")
    st_s, yraw = _sample_update(
        dec[:, :heads].reshape(-1), state_ssm[0].reshape(nbt, d, STATE), xthl, bmat, cmat,
        heads=heads, name="sample_update")
    y_s = _sample_back(yraw, xsd, gate, ysc, x_sample[:, 0, :], nssd, wout32, nfin, name="sample_back")

    return (
        y_p.reshape(nb, seq, d),
        y_s.reshape(nbt, 1, d),
        st_p.reshape(1, nb, heads, HEADDIM, STATE),
        xt_p[:, 5:8, :][None],
        vt_p[:, 6:8, :][None],
        st_s.reshape(1, nbt, heads, HEADDIM, STATE),
        cst_n.reshape(1, nbt, 3, d_xbc),
        sst_n.reshape(1, nbt, 2, d),
    )
```

```python
import functools

import jax
import jax.numpy as jnp
from jax import lax
from jax.experimental import pallas as pl
from jax.experimental.pallas import tpu as pltpu

F32 = jnp.float32
BF16 = jnp.bfloat16
U32 = jnp.uint32
EPS = 1e-5

HEADDIM = 64
GROUPS = 4
STATE = 128
CHUNK = 128
LANES = 128
TW = 256
TN = 1024
NORM_ROWS = 256

VMEM_LIMIT_INPROJ = 52 * 1024 * 1024
VMEM_LIMIT_MIXER = 58 * 1024 * 1024
VMEM_LIMIT_SMALL = 48 * 1024 * 1024


def _silu(x):
    h = 0.5 * x
    return h + h * jnp.tanh(h)


def _softplus(x):
    return jnp.maximum(x, 0.0) + jnp.log1p(jnp.exp(-jnp.abs(x)))


def _split_hi_lo(x):
    hi = x.astype(BF16)
    lo = (x - hi.astype(F32)).astype(BF16)
    return hi, lo


def _pack(x_bf16):
    return pltpu.bitcast(x_bf16, U32)


def _unpack(x_u32):
    return pltpu.bitcast(x_u32, BF16)


def _pack_rows_body(w_ref, o_ref):
    o_ref[...] = _pack(w_ref[...].astype(BF16))


def _pack_rows(w, *, rows_per_step, name):
    k, n = w.shape
    return pl.pallas_call(
        _pack_rows_body,
        grid=(k // rows_per_step,),
        in_specs=[pl.BlockSpec((rows_per_step, n), lambda i: (i, 0))],
        out_specs=pl.BlockSpec((rows_per_step // 2, n), lambda i: (i, 0)),
        out_shape=jax.ShapeDtypeStruct((k // 2, n), U32),
        compiler_params=pltpu.CompilerParams(dimension_semantics=("arbitrary",)),
        name=name,
    )(w)


def _norm_body(xp_ref, xs_ref, nw_ref, wdt_ref, hn_ref, dt_ref, wdt_s, *, n_prompt_steps, heads):
    i = pl.program_id(0)
    rg = 64

    @pl.when(i == 0)
    def _():
        wdt_s[...] = jnp.zeros_like(wdt_s)
        wdt_s[0:heads, :] = wdt_ref[...].astype(BF16)

    def norm_from(src_ref):
        nw = nw_ref[...]
        for r in range(NORM_ROWS // rg):
            x = src_ref[r * rg:(r + 1) * rg, :]
            ms = jnp.mean(x * x, axis=-1, keepdims=True)
            hn = (x * lax.rsqrt(ms + EPS) * nw).astype(BF16)
            hn_ref[r * (rg // 2):(r + 1) * (rg // 2), :] = _pack(hn)

    @pl.when(i < n_prompt_steps)
    def _():
        norm_from(xp_ref)

    @pl.when(i >= n_prompt_steps)
    def _():
        norm_from(xs_ref)

    dt_ref[...] = lax.dot_general(_unpack(hn_ref[...]), wdt_s[...], (((1,), (1,)), ((), ())),
                                  preferred_element_type=F32)


def _norm(xp, xs, nw, w_t, *, dt_row0, heads, name):
    rows_p, d = xp.shape
    rows_s = xs.shape[0]
    assert rows_p % NORM_ROWS == 0 and rows_s == NORM_ROWS and dt_row0 % heads == 0
    n_prompt_steps = rows_p // NORM_ROWS
    rows = rows_p + rows_s
    return pl.pallas_call(
        functools.partial(_norm_body, n_prompt_steps=n_prompt_steps, heads=heads),
        grid=(n_prompt_steps + 1,),
        in_specs=[
            pl.BlockSpec((NORM_ROWS, d), lambda i: (jnp.minimum(i, n_prompt_steps - 1), 0)),
            pl.BlockSpec((NORM_ROWS, d), lambda i: (0, 0)),
            pl.BlockSpec((1, d), lambda i: (0, 0)),
            pl.BlockSpec((heads, d), lambda i: (dt_row0 // heads, 0)),
        ],
        out_specs=[
            pl.BlockSpec((NORM_ROWS // 2, d), lambda i: (i, 0)),
            pl.BlockSpec((NORM_ROWS, LANES), lambda i: (i, 0)),
        ],
        out_shape=[
            jax.ShapeDtypeStruct((rows // 2, d), U32),
            jax.ShapeDtypeStruct((rows, LANES), F32),
        ],
        scratch_shapes=[pltpu.VMEM((LANES, d), BF16)],
        compiler_params=pltpu.CompilerParams(dimension_semantics=("arbitrary",)),
        name=name,
    )(xp, xs, nw, w_t)


W_ROW_UNIT = 32


def _inproj_body(hn_ref, w_ref, p_ref, wbf_s):
    rg = 64

    @pl.when(pl.program_id(1) == 0)
    def _():
        def body(r, carry):
            sl = pl.ds(pl.multiple_of(r * rg, rg), rg)
            wbf_s[sl, :] = w_ref[sl, :].astype(BF16)
            return carry

        lax.fori_loop(0, TN // rg, body, 0)

    acc = lax.dot_general(_unpack(hn_ref[...]), wbf_s[...], (((1,), (1,)), ((), ())),
                          preferred_element_type=F32)
    p_ref[...] = _pack(acc.astype(BF16))


def _inproj(hn32, w_t, *, tm, row_starts, name):
    rows2, d = hn32.shape
    nj = len(row_starts)
    ni = (2 * rows2) // tm
    assert all(r % W_ROW_UNIT == 0 for r in row_starts)
    starts = jnp.asarray([r // W_ROW_UNIT for r in row_starts], jnp.int32)

    grid_spec = pltpu.PrefetchScalarGridSpec(
        num_scalar_prefetch=1,
        grid=(nj, ni),
        in_specs=[
            pl.BlockSpec((tm // 2, d), lambda j, i, st: (i, 0)),
            pl.BlockSpec((pl.Element(TN), pl.Element(d)), lambda j, i, st: (st[j] * W_ROW_UNIT, 0)),
        ],
        out_specs=pl.BlockSpec((tm // 2, TN), lambda j, i, st: (i, j)),
        scratch_shapes=[pltpu.VMEM((TN, d), BF16)],
    )

    def body(st_ref, *refs):
        _inproj_body(*refs)

    return pl.pallas_call(
        body,
        grid_spec=grid_spec,
        out_shape=jax.ShapeDtypeStruct((rows2, nj * TN), U32),
        compiler_params=pltpu.CompilerParams(
            dimension_semantics=("arbitrary", "arbitrary"),
            vmem_limit_bytes=VMEM_LIMIT_INPROJ),
        name=name,
    )(starts, hn32, w_t)


def _conv4(ext, w_ref, cols):
    s1 = pltpu.roll(ext, 1, 0)
    q = ext * w_ref[1:2, cols] + s1 * w_ref[0:1, cols]
    out = ext * w_ref[3:4, cols] + s1 * w_ref[2:3, cols] + pltpu.roll(q, 2, 0)
    return out[8:, :]


def _conv3(ext, w_ref, cols):
    s1 = pltpu.roll(ext, 1, 0)
    q = ext * w_ref[1:2, cols] + s1 * w_ref[0:1, cols]
    out = ext * w_ref[2:3, cols] + pltpu.roll(q, 1, 0)
    return out[8:, :]


def _mixer_body(zs_ref, zc_ref, b_ref, c_ref, h_ref, xr_ref, bcr_ref, dtr_ref, u_ref,
                s0_ref, xt0_ref, vt0_ref,
                cw_ref, cbias_ref, dtb_ref, alog_ref, dexp_ref, nssd_ref, scw_ref, nsc_ref,
                wout_ref, nfin_ref, e2_ref,
                y_ref, st_ref, stT_ref, xt_ref, vt_ref,
                state_s, xh_s, vh_s, xc_s, exp_s, cbm_s, bt_s, at_s, xdt_s, xd_s, ypre_s,
                ytmp_s, ymix_s, ymixp_s,
                *, n_pad, d, heads, nc, total):
    s = pl.program_id(0)
    c = s % nc
    d_gn = GROUPS * STATE
    d_xbc = d + 2 * d_gn
    gw = d // GROUPS

    @pl.when(s == 0)
    def _zero():
        ymix_s[...] = jnp.zeros_like(ymix_s)

    @pl.when(c == 0)
    def _init():
        state_s[...] = s0_ref[...]
        xh_s[...] = xt0_ref[...]
        vh_s[...] = vt0_ref[...]

    ymixp_s[...] = ymix_s[...]
    n_pieces = d // TW
    usq = [jnp.zeros((CHUNK, TW), F32)]

    def outproj_piece(n):
        cols = slice(n * TW, (n + 1) * TW)
        u = u_ref[:, cols] + jnp.dot(ymixp_s[...], _unpack(wout_ref[:, cols]),
                                     preferred_element_type=F32)
        y_ref[:, cols] = u
        usq[0] = usq[0] + u * u

    rowi = lax.broadcasted_iota(jnp.int32, (CHUNK, LANES), 0)
    coli = lax.broadcasted_iota(jnp.int32, (CHUNK, LANES), 1)

    for t in range(d_xbc // TW):
        if t % 5 == 0:
            outproj_piece(t // 5)
        c0 = t * TW
        cols = slice(c0, c0 + TW)
        if c0 < d:
            x = _unpack(xr_ref[:, cols]).astype(F32)
        else:
            x = _unpack(bcr_ref[:, c0 - d:c0 - d + TW]).astype(F32)
        ext = jnp.concatenate([xh_s[:, cols], x], axis=0)
        xc_s[:, cols] = _silu(_conv4(ext, cw_ref, cols) + cbias_ref[0:1, cols])
        xh_s[:, cols] = x[CHUNK - 8:CHUNK, :]

    dtv = dtr_ref[...] + dtb_ref[...]
    dt = _softplus(dtv)
    if n_pad:
        dt = jnp.where(rowi >= n_pad, dt, 0.0)
    dta = dt * (-jnp.exp(alog_ref[...]))
    tri = (rowi >= coli).astype(F32)
    a_cs = jnp.dot(tri, dta, precision=lax.Precision.HIGHEST, preferred_element_type=F32)
    at_s[...] = a_cs.T
    stack = jnp.concatenate(
        [dt, jnp.exp(a_cs), jnp.exp(a_cs[CHUNK - 1:CHUNK, :] - a_cs)], axis=0)
    hi, lo = _split_hi_lo(stack)
    exp_s[...] = jnp.dot(jnp.concatenate([hi, lo], axis=1), e2_ref[...],
                         preferred_element_type=F32)

    for g in range(GROUPS):
        bg = xc_s[:, d + g * STATE:d + (g + 1) * STATE]
        cg = xc_s[:, d + d_gn + g * STATE:d + d_gn + (g + 1) * STATE]
        cbm_s[g] = lax.dot_general(cg.astype(BF16), bg.astype(BF16),
                                   (((1,), (1,)), ((), ())), preferred_element_type=F32)
        bt_s[g] = bg.T.astype(BF16)

    outproj_piece(3)

    for t in range(d // TW):
        cols = slice(t * TW, (t + 1) * TW)
        xdt = xc_s[:, cols] * exp_s[0:CHUNK, cols]
        xdt_s[:, cols] = xdt.astype(BF16)
        xd_s[:, cols] = (xdt * exp_s[2 * CHUNK:3 * CHUNK, cols]).astype(BF16)

    cdec = exp_s[2 * CHUNK - 1:2 * CHUNK, :]
    for g in range(GROUPS):
        cols = slice(g * gw, (g + 1) * gw)
        s_prev = state_s[:, cols]
        cg = xc_s[:, d + d_gn + g * STATE:d + d_gn + (g + 1) * STATE].astype(BF16)
        yoff = jnp.dot(cg, s_prev.astype(BF16), preferred_element_type=F32)
        ypre_s[:, cols] = yoff * exp_s[CHUNK:2 * CHUNK, cols]
        state_s[:, cols] = s_prev * cdec[:, cols] + jnp.dot(
            bt_s[g], xd_s[:, cols], preferred_element_type=F32)

    causal = rowi >= coli
    lane_lo = coli < HEADDIM
    ss = jnp.zeros((CHUNK, LANES), F32)
    for pr in range(heads // 2):
        if pr % (heads // 4) == 0:
            outproj_piece(4 + pr // (heads // 4))
        g = (2 * pr * HEADDIM) // gw
        ms_ = []
        for hh in (2 * pr, 2 * pr + 1):
            rowb = jnp.broadcast_to(at_s[hh:hh + 1, :], (CHUNK, CHUNK))
            seg = rowb.T - rowb
            decay = jnp.exp(jnp.where(causal, seg, -jnp.inf))
            ms_.append((cbm_s[g] * decay).astype(BF16))
        lhs = jnp.concatenate(ms_, axis=1)
        cols = slice(pr * LANES, (pr + 1) * LANES)
        xp = xdt_s[:, cols]
        zero = jnp.zeros_like(xp)
        rhs = jnp.concatenate([jnp.where(lane_lo, xp, zero), jnp.where(lane_lo, zero, xp)], axis=0)
        y = jnp.dot(lhs, rhs, preferred_element_type=F32)
        y = y + ypre_s[:, cols] + dexp_ref[0:1, cols] * xc_s[:, cols]
        y = y * _silu(_unpack(zs_ref[:, cols]).astype(F32))
        ytmp_s[:, cols] = y
        ss = ss + y * y
    rs = lax.rsqrt(jnp.sum(ss, axis=-1, keepdims=True) / d + EPS)
    for t in range(d // TW):
        cols = slice(t * TW, (t + 1) * TW)
        ymix_s[:, cols] = (ytmp_s[:, cols] * rs * nssd_ref[0:1, cols]).astype(BF16)

    ss = jnp.zeros((CHUNK, TW), F32)
    for t in range(d // TW):
        if t % 4 == 0:
            outproj_piece(6 + t // 4)
        cols = slice(t * TW, (t + 1) * TW)
        v = _unpack(c_ref[:, cols]).astype(F32) * _unpack(h_ref[:, cols]).astype(F32)
        ext = jnp.concatenate([vh_s[:, cols], v], axis=0)
        ysc = _unpack(b_ref[:, cols]).astype(F32) * _conv3(ext, scw_ref, cols)
        ysc = ysc * _silu(_unpack(zc_ref[:, cols]).astype(F32))
        ytmp_s[:, cols] = ysc
        vh_s[:, cols] = v[CHUNK - 8:CHUNK, :]
        ss = ss + ysc * ysc
    rs = lax.rsqrt(jnp.sum(ss, axis=-1, keepdims=True) / d + EPS)
    for t in range(d // TW):
        cols = slice(t * TW, (t + 1) * TW)
        ymix_s[:, d + t * TW:d + (t + 1) * TW] = (
            ytmp_s[:, cols] * rs * nsc_ref[0:1, cols]).astype(BF16)

    assert n_pieces == 8
    rs = lax.rsqrt(jnp.sum(usq[0], axis=-1, keepdims=True) / d + EPS)
    for t in range(d // TW):
        cols = slice(t * TW, (t + 1) * TW)
        y_ref[:, cols] = y_ref[:, cols] * rs * nfin_ref[0:1, cols]

    @pl.when(jnp.logical_and(c == nc - 1, s < total))
    def _fin():
        stT_ref[0] = state_s[...]
        for t in range(d // LANES):
            st_ref[0, t * LANES:(t + 1) * LANES, :] = state_s[:, t * LANES:(t + 1) * LANES].T
        xt_ref[0] = xh_s[...]
        vt_ref[0] = vh_s[...]


def _const_spec(shape):
    nd = len(shape)
    return pl.BlockSpec(shape, lambda s: (0,) * nd)


def _mixer(p32, dtr, u, s0, xt0, vt0, wts, *, nb, nc, row_block0, n_pad, name):
    d = u.shape[1]
    d_gn = GROUPS * STATE
    d_xbc = d + 2 * d_gn
    heads = d // HEADDIM
    total = nb * nc
    (cw, cbias, dtb, alog, dexp, nssd, scw, nsc, wout32, nfin, e2) = wts

    def rowmap(colblk):
        return lambda s: (row_block0 + jnp.minimum(s, total - 1), colblk)

    prev = lambda s: (jnp.maximum(s - 1, 0), 0)
    bmap = lambda s: (jnp.minimum(s // nc, nb - 1), 0, 0)

    seg = lambda k: pl.BlockSpec((CHUNK // 2, d), rowmap(k))
    in_specs = [
        seg(0), seg(1), seg(2), seg(3), seg(4), seg(5),
        pl.BlockSpec((CHUNK // 2, 2 * d_gn), rowmap((6 * d) // (2 * d_gn))),
        pl.BlockSpec((CHUNK, LANES), rowmap(0)),
        pl.BlockSpec((CHUNK, d), prev),
        _const_spec((STATE, d)), _const_spec((8, d_xbc)), _const_spec((8, d)),
        _const_spec(cw.shape), _const_spec(cbias.shape), _const_spec(dtb.shape),
        _const_spec(alog.shape), _const_spec(dexp.shape), _const_spec(nssd.shape),
        _const_spec(scw.shape), _const_spec(nsc.shape),
        pl.BlockSpec(wout32.shape, lambda s: (0, 0), pipeline_mode=pl.Buffered(1)),
        _const_spec(nfin.shape),
        pl.BlockSpec(e2.shape, lambda s: (0, 0), pipeline_mode=pl.Buffered(1)),
    ]
    out_specs = [
        pl.BlockSpec((CHUNK, d), prev),
        pl.BlockSpec((1, d, STATE), bmap),
        pl.BlockSpec((1, STATE, d), bmap),
        pl.BlockSpec((1, 8, d_xbc), bmap),
        pl.BlockSpec((1, 8, d), bmap),
    ]
    out_shape = [
        jax.ShapeDtypeStruct((total * CHUNK, d), F32),
        jax.ShapeDtypeStruct((nb, d, STATE), F32),
        jax.ShapeDtypeStruct((nb, STATE, d), F32),
        jax.ShapeDtypeStruct((nb, 8, d_xbc), F32),
        jax.ShapeDtypeStruct((nb, 8, d), F32),
    ]
    scratch = [
        pltpu.VMEM((STATE, d), F32),
        pltpu.VMEM((8, d_xbc), F32),
        pltpu.VMEM((8, d), F32),
        pltpu.VMEM((CHUNK, d_xbc), F32),
        pltpu.VMEM((3 * CHUNK, d), F32),
        pltpu.VMEM((GROUPS, CHUNK, CHUNK), F32),
        pltpu.VMEM((GROUPS, STATE, CHUNK), BF16),
        pltpu.VMEM((LANES, CHUNK), F32),
        pltpu.VMEM((CHUNK, d), BF16),
        pltpu.VMEM((CHUNK, d), BF16),
        pltpu.VMEM((CHUNK, d), F32),
        pltpu.VMEM((CHUNK, d), F32),
        pltpu.VMEM((CHUNK, 2 * d), BF16),
        pltpu.VMEM((CHUNK, 2 * d), BF16),
    ]
    return pl.pallas_call(
        functools.partial(_mixer_body, n_pad=n_pad, d=d, heads=heads, nc=nc, total=total),
        grid=(total + 1,),
        in_specs=in_specs,
        out_specs=out_specs,
        out_shape=out_shape,
        scratch_shapes=scratch,
        compiler_params=pltpu.CompilerParams(
            dimension_semantics=("arbitrary",),
            vmem_limit_bytes=VMEM_LIMIT_MIXER),
        name=name,
    )(p32, p32, p32, p32, p32, p32, p32, dtr, u, s0, xt0, vt0, cw, cbias, dtb, alog, dexp, nssd,
      scw, nsc, wout32, nfin, e2)


def _sample_front_body(zs_ref, zc_ref, b_ref, c_ref, h_ref, xr_ref, bcr_ref, dtr_ref,
                       cst_ref, sst_ref,
                       cw_ref, cbias_ref, dtb_ref, alog_ref, dexp_ref, scw_ref, nsc_ref, e2_ref,
                       cst_o, sst_o, xthl_o, bmat_o, cmat_o, dec_o, xsd_o, gate_o, ysc_o,
                       xc_s, ytmp_s, *, d):
    d_gn = GROUPS * STATE
    d_xbc = d + 2 * d_gn
    nbt = cst_ref.shape[1]

    for t in range(d_xbc // TW):
        c0 = t * TW
        cols = slice(c0, c0 + TW)
        if c0 < d:
            x = _unpack(xr_ref[:, cols]).astype(F32)
        else:
            x = _unpack(bcr_ref[:, c0 - d:c0 - d + TW]).astype(F32)
        x0 = cst_ref[0, :, cols]
        x1 = cst_ref[1, :, cols]
        x2 = cst_ref[2, :, cols]
        acc = (x0 * cw_ref[0:1, cols] + x1 * cw_ref[1:2, cols] + x2 * cw_ref[2:3, cols]
               + x * cw_ref[3:4, cols] + cbias_ref[0:1, cols])
        xc_s[:, cols] = _silu(acc)
        cst_o[0, :, cols] = x1
        cst_o[1, :, cols] = x2
        cst_o[2, :, cols] = x

    dt = _softplus(dtr_ref[...] + dtb_ref[...])
    dec_o[...] = jnp.exp(dt * (-jnp.exp(alog_ref[...])))
    hi, lo = _split_hi_lo(dt)
    dt_exp = jnp.dot(jnp.concatenate([hi, lo], axis=1), e2_ref[...], preferred_element_type=F32)

    bmat_o[...] = xc_s[:, d:d + d_gn].astype(BF16)
    cmat_o[...] = xc_s[:, d + d_gn:d + 2 * d_gn]

    for t in range(d // LANES):
        cols = slice(t * LANES, (t + 1) * LANES)
        xs = xc_s[:, cols]
        xsd_o[:, cols] = xs * dexp_ref[0:1, cols]
        gate_o[:, cols] = _silu(_unpack(zs_ref[:, cols]).astype(F32))
        xdt_t = (xs * dt_exp[:, cols]).T
        hi, lo = _split_hi_lo(xdt_t)
        xthl_o[cols, 0:nbt] = hi
        xthl_o[cols, nbt:2 * nbt] = lo

    ss = jnp.zeros((nbt, TW), F32)
    for t in range(d // TW):
        c0 = t * TW
        cols = slice(c0, c0 + TW)
        v = _unpack(c_ref[:, cols]).astype(F32) * _unpack(h_ref[:, cols]).astype(F32)
        v0 = sst_ref[:, c0:c0 + TW]
        v1 = sst_ref[:, d + c0:d + c0 + TW]
        acc = v0 * scw_ref[0:1, cols] + v1 * scw_ref[1:2, cols] + v * scw_ref[2:3, cols]
        ysc = _unpack(b_ref[:, cols]).astype(F32) * acc
        ysc = ysc * _silu(_unpack(zc_ref[:, cols]).astype(F32))
        ytmp_s[:, cols] = ysc
        sst_o[:, c0:c0 + TW] = v1
        sst_o[:, d + c0:d + c0 + TW] = v
        ss = ss + ysc * ysc
    rs = lax.rsqrt(jnp.sum(ss, axis=-1, keepdims=True) / d + EPS)
    for t in range(d // TW):
        cols = slice(t * TW, (t + 1) * TW)
        ysc_o[:, cols] = (ytmp_s[:, cols] * rs * nsc_ref[0:1, cols]).astype(BF16)


def _sample_front(p32, dtr, cst, sst, wts, *, row_block, d, name):
    d_gn = GROUPS * STATE
    d_xbc = d + 2 * d_gn
    nbt = cst.shape[1]
    (cw, cbias, dtb, alog, dexp, scw, nsc, e2) = wts
    seg = lambda k: pl.BlockSpec((nbt // 2, d), lambda i: (row_block, k))
    full = lambda a: pl.BlockSpec(a.shape, lambda i: (0,) * a.ndim)
    in_specs = [
        seg(0), seg(1), seg(2), seg(3), seg(4), seg(5),
        pl.BlockSpec((nbt // 2, 2 * d_gn), lambda i: (row_block, (6 * d) // (2 * d_gn))),
        pl.BlockSpec((nbt, LANES), lambda i: (row_block, 0)),
        full(cst), full(sst),
        full(cw), full(cbias), full(dtb), full(alog), full(dexp), full(scw), full(nsc), full(e2),
    ]
    out_shape = [
        jax.ShapeDtypeStruct(cst.shape, F32),
        jax.ShapeDtypeStruct(sst.shape, F32),
        jax.ShapeDtypeStruct((d, 2 * nbt), BF16),
        jax.ShapeDtypeStruct((nbt, d_gn), BF16),
        jax.ShapeDtypeStruct((nbt, d_gn), F32),
        jax.ShapeDtypeStruct((nbt, LANES), F32),
        jax.ShapeDtypeStruct((nbt, d), F32),
        jax.ShapeDtypeStruct((nbt, d), F32),
        jax.ShapeDtypeStruct((nbt, d), BF16),
    ]
    out_specs = [pl.BlockSpec(s.shape, lambda i, nd=len(s.shape): (0,) * nd) for s in out_shape]
    return pl.pallas_call(
        functools.partial(_sample_front_body, d=d),
        grid=(1,),
        in_specs=in_specs,
        out_specs=out_specs,
        out_shape=out_shape,
        scratch_shapes=[pltpu.VMEM((nbt, d_xbc), F32), pltpu.VMEM((nbt, d), F32)],
        compiler_params=pltpu.CompilerParams(vmem_limit_bytes=VMEM_LIMIT_SMALL),
        name=name,
    )(p32, p32, p32, p32, p32, p32, p32, dtr, cst, sst, cw, cbias, dtb, alog, dexp, scw, nsc, e2)


def _sample_update_body(dec_ref, st_ref, xthl_ref, bmat_ref, cmat_ref, so_ref, yraw_ref, *, heads):
    b = pl.program_id(0)
    nbt = bmat_ref.shape[0]
    d = xthl_ref.shape[0]
    gw = d // GROUPS
    hpg = heads // GROUPS
    rb = 16

    @pl.when(b == 0)
    def _():
        yraw_ref[...] = jnp.zeros_like(yraw_ref)

    rowi = lax.broadcasted_iota(jnp.int32, (nbt, STATE), 0)
    mask_b = rowi == b
    b16 = pl.multiple_of((b // rb) * rb, rb)
    row16 = (lax.broadcasted_iota(jnp.int32, (rb, STATE), 0) + b16) == b
    for g in range(GROUPS):
        bm = bmat_ref[:, g * STATE:(g + 1) * STATE]
        r = jnp.where(mask_b, bm, jnp.zeros_like(bm))
        upd = jnp.dot(xthl_ref[g * gw:(g + 1) * gw, :], jnp.concatenate([r, r], axis=0),
                      preferred_element_type=F32)
        parts = []
        for hh in range(hpg):
            r0 = g * gw + hh * HEADDIM
            parts.append(st_ref[0, r0:r0 + HEADDIM, :] * dec_ref[b * heads + g * hpg + hh])
        s_new = jnp.concatenate(parts, axis=0) + upd
        so_ref[0, g * gw:(g + 1) * gw, :] = s_new
        c16 = jnp.where(row16, cmat_ref[pl.ds(b16, rb), g * STATE:(g + 1) * STATE], 0.0)
        yg = lax.dot_general(c16.astype(BF16), s_new.astype(BF16),
                             (((1,), (1,)), ((), ())), preferred_element_type=F32)
        yraw_ref[pl.ds(b16, rb), g * gw:(g + 1) * gw] += yg


def _sample_update(dec_flat, state, xthl, bmat, cmat, *, heads, name):
    nbt, d, n = state.shape
    return pl.pallas_call(
        functools.partial(_sample_update_body, heads=heads),
        grid=(nbt,),
        in_specs=[
            pl.BlockSpec(memory_space=pltpu.SMEM),
            pl.BlockSpec((1, d, n), lambda b: (b, 0, 0)),
            pl.BlockSpec(xthl.shape, lambda b: (0, 0)),
            pl.BlockSpec(bmat.shape, lambda b: (0, 0)),
            pl.BlockSpec(cmat.shape, lambda b: (0, 0)),
        ],
        out_specs=[
            pl.BlockSpec((1, d, n), lambda b: (b, 0, 0)),
            pl.BlockSpec((nbt, d), lambda b: (0, 0)),
        ],
        out_shape=[
            jax.ShapeDtypeStruct(state.shape, F32),
            jax.ShapeDtypeStruct((nbt, d), F32),
        ],
        compiler_params=pltpu.CompilerParams(
            dimension_semantics=("arbitrary",), vmem_limit_bytes=VMEM_LIMIT_SMALL),
        name=name,
    )(dec_flat, state, xthl, bmat, cmat)


def _sample_back_body(yraw_ref, xsd_ref, gate_ref, ysc_ref, u_ref, nssd_ref, wout_ref, nfin_ref,
                      y_ref, ymix_s, *, d):
    y = (yraw_ref[...] + xsd_ref[...]) * gate_ref[...]
    rs = lax.rsqrt(jnp.mean(y * y, axis=-1, keepdims=True) + EPS)
    ymix_s[:, 0:d] = (y * rs * nssd_ref[...]).astype(BF16)
    ymix_s[:, d:2 * d] = ysc_ref[...]
    out = jnp.dot(ymix_s[...], _unpack(wout_ref[...]), preferred_element_type=F32)
    u = u_ref[...] + out
    ms = jnp.mean(u * u, axis=-1, keepdims=True)
    y_ref[...] = u * lax.rsqrt(ms + EPS) * nfin_ref[...]


def _sample_back(yraw, xsd, gate, ysc, u, nssd, wout32, nfin, *, name):
    nbt, d = u.shape
    args = (yraw, xsd, gate, ysc, u, nssd, wout32, nfin)
    return pl.pallas_call(
        functools.partial(_sample_back_body, d=d),
        grid=(1,),
        in_specs=[pl.BlockSpec(a.shape, lambda i: (0,) * a.ndim) for a in args],
        out_specs=pl.BlockSpec((nbt, d), lambda i: (0, 0)),
        out_shape=jax.ShapeDtypeStruct((nbt, d), F32),
        scratch_shapes=[pltpu.VMEM((nbt, 2 * d), BF16)],
        compiler_params=pltpu.CompilerParams(vmem_limit_bytes=VMEM_LIMIT_SMALL),
        name=name,
    )(*args)


def kernel(x_prompt, x_sample, state_ssm, state_ssd_conv, state_short_conv, meta_tokens, norm_w, w_in,
           conv_ssd_w, conv_ssd_b, dt_bias, a_log, d_skip, ssd_norm_w, conv_sc_w, sc_norm_w, w_out,
           final_norm_w):
    nb, seq, d = x_prompt.shape
    nbt = x_sample.shape[0]
    depth = norm_w.shape[0]
    assert depth == 1 and x_sample.shape[1] == 1
    heads = d // HEADDIM
    d_gn = GROUPS * STATE
    d_xbc = d + 2 * d_gn
    n_meta = meta_tokens.shape[0]
    assert seq % CHUNK == 0 and nbt == CHUNK and n_meta <= CHUNK and heads <= LANES
    nc = seq // CHUNK
    rows_p = nb * seq
    rows = rows_p + CHUNK + nbt

    o_dt = d + d_xbc
    o_sc = o_dt + heads
    assert d % TN == 0 and d_xbc % TN == 0 and (2 * d_gn) == TN and heads == W_ROW_UNIT
    row_starts = ([t * TN for t in range(d // TN)]
                  + [o_sc + t * TN for t in range(4 * d // TN)]
                  + [d + t * TN for t in range(d_xbc // TN)])

    w_t = w_in[0].T
    nw = norm_w[0][None, :]
    cw = conv_ssd_w[0]
    cbias = conv_ssd_b[0][None, :]
    dtb = jnp.pad(dt_bias[0], (0, LANES - heads))[None, :]
    alog = jnp.pad(a_log[0], (0, LANES - heads))[None, :]
    dexp = jnp.repeat(d_skip[0], HEADDIM)[None, :]
    nssd = ssd_norm_w[0][None, :]
    scw = conv_sc_w[0]
    nsc = sc_norm_w[0][None, :]
    nfin = final_norm_w[None, :]
    k_i = lax.broadcasted_iota(jnp.int32, (2 * LANES, d), 0)
    c_i = lax.broadcasted_iota(jnp.int32, (2 * LANES, d), 1)
    e2 = ((k_i % LANES) == (c_i // HEADDIM)).astype(BF16)

    wout32 = _pack_rows(w_out[0], rows_per_step=512, name="pack_wout")

    xp = x_prompt.reshape(rows_p, d)
    x_small = jnp.concatenate(
        [jnp.zeros((CHUNK - n_meta, d), F32), meta_tokens.astype(F32), x_sample[:, 0, :]], axis=0)
    hn32, dt_raw = _norm(xp, x_small, nw, w_t, dt_row0=o_dt, heads=heads, name="norm")
    tm = rows // 8
    assert rows % 8 == 0 and tm % 16 == 0
    p32 = _inproj(hn32, w_t, tm=tm, row_starts=row_starts, name="inproj")

    mixer_wts = (cw, cbias, dtb, alog, dexp, nssd, scw, nsc, wout32, nfin, e2)
    meta_block = rows_p // CHUNK
    sample_block = meta_block + 1

    zeros_state = jnp.zeros((STATE, d), F32)
    _, _, st_t_meta, xt_meta, vt_meta = _mixer(
        p32, dt_raw, x_small, zeros_state, jnp.zeros((8, d_xbc), F32), jnp.zeros((8, d), F32),
        mixer_wts, nb=1, nc=1, row_block0=meta_block, n_pad=CHUNK - n_meta, name="mixer_meta")

    y_p, st_p, _, xt_p, vt_p = _mixer(
        p32, dt_raw, xp, st_t_meta[0], xt_meta[0], vt_meta[0],
        mixer_wts, nb=nb, nc=nc, row_block0=0, n_pad=0, name="mixer_prompt")

    cst = jnp.transpose(state_ssd_conv[0], (1, 0, 2))
    sst = state_short_conv[0].reshape(nbt, 2 * d)
    (cst_n, sst_n, xthl, bmat, cmat, dec, xsd, gate, ysc) = _sample_front(
        p32, dt_raw, cst, sst, (cw, cbias, dtb, alog, dexp, scw, nsc, e2),
        row_block=sample_block, d=d, name="sample_front")
    st_s, yraw = _sample_update(
        dec[:, :heads].reshape(-1), state_ssm[0].reshape(nbt, d, STATE), xthl, bmat, cmat,
        heads=heads, name="sample_update")
    y_s = _sample_back(yraw, xsd, gate, ysc, x_sample[:, 0, :], nssd, wout32, nfin, name="sample_back")

    return (
        y_p.reshape(nb, seq, d),
        y_s.reshape(nbt, 1, d),
        st_p.reshape(1, nb, heads, HEADDIM, STATE),
        xt_p[:, 5:8, :][None],
        vt_p[:, 6:8, :][None],
        st_s.reshape(1, nbt, heads, HEADDIM, STATE),
        jnp.transpose(cst_n, (1, 0, 2))[None],
        sst_n.reshape(1, nbt, 2, d),
    )
```

```python
import functools

import jax
import jax.numpy as jnp
from jax import lax
from jax.experimental import pallas as pl
from jax.experimental.pallas import tpu as pltpu

F32 = jnp.float32
BF16 = jnp.bfloat16
U32 = jnp.uint32
EPS = 1e-5

HEADDIM = 64
GROUPS = 4
STATE = 128
CHUNK = 128
LANES = 128
TW = 256
TN = 1024
NORM_ROWS = 256

VMEM_LIMIT_INPROJ = 52 * 1024 * 1024
VMEM_LIMIT_MIXER = 58 * 1024 * 1024
VMEM_LIMIT_SMALL = 48 * 1024 * 1024


def _silu(x):
    h = 0.5 * x
    return h + h * jnp.tanh(h)


def _softplus(x):
    return jnp.maximum(x, 0.0) + jnp.log1p(jnp.exp(-jnp.abs(x)))


def _split_hi_lo(x):
    hi = x.astype(BF16)
    lo = (x - hi.astype(F32)).astype(BF16)
    return hi, lo


def _pack(x_bf16):
    return pltpu.bitcast(x_bf16, U32)


def _unpack(x_u32):
    return pltpu.bitcast(x_u32, BF16)


def _pack_rows_body(w_ref, o_ref):
    o_ref[...] = _pack(w_ref[...].astype(BF16))


def _pack_rows(w, *, rows_per_step, name):
    k, n = w.shape
    return pl.pallas_call(
        _pack_rows_body,
        grid=(k // rows_per_step,),
        in_specs=[pl.BlockSpec((rows_per_step, n), lambda i: (i, 0))],
        out_specs=pl.BlockSpec((rows_per_step // 2, n), lambda i: (i, 0)),
        out_shape=jax.ShapeDtypeStruct((k // 2, n), U32),
        compiler_params=pltpu.CompilerParams(dimension_semantics=("arbitrary",)),
        name=name,
    )(w)


def _norm_body(xp_ref, xs_ref, nw_ref, wdt_ref, hn_ref, dt_ref, wdt_s, *, n_prompt_steps, heads):
    i = pl.program_id(0)
    rg = 64

    @pl.when(i == 0)
    def _():
        wdt_s[...] = jnp.zeros_like(wdt_s)
        wdt_s[0:heads, :] = wdt_ref[...].astype(BF16)

    def norm_from(src_ref):
        nw = nw_ref[...]
        for r in range(NORM_ROWS // rg):
            x = src_ref[r * rg:(r + 1) * rg, :]
            ms = jnp.mean(x * x, axis=-1, keepdims=True)
            hn = (x * lax.rsqrt(ms + EPS) * nw).astype(BF16)
            hn_ref[r * (rg // 2):(r + 1) * (rg // 2), :] = _pack(hn)

    @pl.when(i < n_prompt_steps)
    def _():
        norm_from(xp_ref)

    @pl.when(i >= n_prompt_steps)
    def _():
        norm_from(xs_ref)

    dt_ref[...] = lax.dot_general(_unpack(hn_ref[...]), wdt_s[...], (((1,), (1,)), ((), ())),
                                  preferred_element_type=F32)


def _norm(xp, xs, nw, w_t, *, dt_row0, heads, name):
    rows_p, d = xp.shape
    rows_s = xs.shape[0]
    assert rows_p % NORM_ROWS == 0 and rows_s == NORM_ROWS and dt_row0 % heads == 0
    n_prompt_steps = rows_p // NORM_ROWS
    rows = rows_p + rows_s
    return pl.pallas_call(
        functools.partial(_norm_body, n_prompt_steps=n_prompt_steps, heads=heads),
        grid=(n_prompt_steps + 1,),
        in_specs=[
            pl.BlockSpec((NORM_ROWS, d), lambda i: (jnp.minimum(i, n_prompt_steps - 1), 0)),
            pl.BlockSpec((NORM_ROWS, d), lambda i: (0, 0)),
            pl.BlockSpec((1, d), lambda i: (0, 0)),
            pl.BlockSpec((heads, d), lambda i: (dt_row0 // heads, 0)),
        ],
        out_specs=[
            pl.BlockSpec((NORM_ROWS // 2, d), lambda i: (i, 0)),
            pl.BlockSpec((NORM_ROWS, LANES), lambda i: (i, 0)),
        ],
        out_shape=[
            jax.ShapeDtypeStruct((rows // 2, d), U32),
            jax.ShapeDtypeStruct((rows, LANES), F32),
        ],
        scratch_shapes=[pltpu.VMEM((LANES, d), BF16)],
        compiler_params=pltpu.CompilerParams(dimension_semantics=("arbitrary",)),
        name=name,
    )(xp, xs, nw, w_t)


W_ROW_UNIT = 32


def _inproj_body(hn_ref, w_ref, p_ref, wbf_s):
    rg = 64

    @pl.when(pl.program_id(1) == 0)
    def _():
        def body(r, carry):
            sl = pl.ds(pl.multiple_of(r * rg, rg), rg)
            wbf_s[sl, :] = w_ref[sl, :].astype(BF16)
            return carry

        lax.fori_loop(0, TN // rg, body, 0)

    acc = lax.dot_general(_unpack(hn_ref[...]), wbf_s[...], (((1,), (1,)), ((), ())),
                          preferred_element_type=F32)
    p_ref[...] = _pack(acc.astype(BF16))


def _inproj(hn32, w_t, *, tm, row_starts, name):
    rows2, d = hn32.shape
    nj = len(row_starts)
    ni = (2 * rows2) // tm
    assert all(r % W_ROW_UNIT == 0 for r in row_starts)
    starts = jnp.asarray([r // W_ROW_UNIT for r in row_starts], jnp.int32)

    grid_spec = pltpu.PrefetchScalarGridSpec(
        num_scalar_prefetch=1,
        grid=(nj, ni),
        in_specs=[
            pl.BlockSpec((tm // 2, d), lambda j, i, st: (i, 0)),
            pl.BlockSpec((pl.Element(TN), pl.Element(d)), lambda j, i, st: (st[j] * W_ROW_UNIT, 0)),
        ],
        out_specs=pl.BlockSpec((tm // 2, TN), lambda j, i, st: (i, j)),
        scratch_shapes=[pltpu.VMEM((TN, d), BF16)],
    )

    def body(st_ref, *refs):
        _inproj_body(*refs)

    return pl.pallas_call(
        body,
        grid_spec=grid_spec,
        out_shape=jax.ShapeDtypeStruct((rows2, nj * TN), U32),
        compiler_params=pltpu.CompilerParams(
            dimension_semantics=("arbitrary", "arbitrary"),
            vmem_limit_bytes=VMEM_LIMIT_INPROJ),
        name=name,
    )(starts, hn32, w_t)


def _conv4(ext, w_ref, cols):
    s1 = pltpu.roll(ext, 1, 0)
    q = ext * w_ref[1:2, cols] + s1 * w_ref[0:1, cols]
    out = ext * w_ref[3:4, cols] + s1 * w_ref[2:3, cols] + pltpu.roll(q, 2, 0)
    return out[8:, :]


def _conv3(ext, w_ref, cols):
    s1 = pltpu.roll(ext, 1, 0)
    q = ext * w_ref[1:2, cols] + s1 * w_ref[0:1, cols]
    out = ext * w_ref[2:3, cols] + pltpu.roll(q, 1, 0)
    return out[8:, :]


def _mixer_body(zs_ref, zc_ref, b_ref, c_ref, h_ref, xr_ref, bcr_ref, dtr_ref, u_ref,
                s0_ref, xt0_ref, vt0_ref,
                cw_ref, cbias_ref, dtb_ref, alog_ref, dexp_ref, nssd_ref, scw_ref, nsc_ref,
                wout_ref, nfin_ref, e2_ref,
                y_ref, st_ref, stT_ref, xt_ref, vt_ref,
                state_s, xh_s, vh_s, xc_s, exp_s, cbm_s, bt_s, at_s, xdt_s, xd_s, ypre_s,
                ytmp_s, ymix_s, ymixp_s,
                *, n_pad, d, heads, nc, total, with_outproj):
    s = pl.program_id(0)
    c = s % nc
    d_gn = GROUPS * STATE
    d_xbc = d + 2 * d_gn
    gw = d // GROUPS

    @pl.when(s == 0)
    def _zero():
        ymix_s[...] = jnp.zeros_like(ymix_s)

    @pl.when(c == 0)
    def _init():
        state_s[...] = s0_ref[...]
        xh_s[...] = xt0_ref[...]
        vh_s[...] = vt0_ref[...]

    if with_outproj:
        ymixp_s[...] = ymix_s[...]
    n_pieces = d // TW
    usq = [jnp.zeros((CHUNK, TW), F32)]

    def outproj_piece(n):
        if not with_outproj:
            return
        cols = slice(n * TW, (n + 1) * TW)
        u = u_ref[:, cols] + jnp.dot(ymixp_s[...], _unpack(wout_ref[:, cols]),
                                     preferred_element_type=F32)
        y_ref[:, cols] = u
        usq[0] = usq[0] + u * u

    rowi = lax.broadcasted_iota(jnp.int32, (CHUNK, LANES), 0)
    coli = lax.broadcasted_iota(jnp.int32, (CHUNK, LANES), 1)

    for t in range(d_xbc // TW):
        if t % 5 == 0:
            outproj_piece(t // 5)
        c0 = t * TW
        cols = slice(c0, c0 + TW)
        if c0 < d:
            x = _unpack(xr_ref[:, cols]).astype(F32)
        else:
            x = _unpack(bcr_ref[:, c0 - d:c0 - d + TW]).astype(F32)
        ext = jnp.concatenate([xh_s[:, cols], x], axis=0)
        xc_s[:, cols] = _silu(_conv4(ext, cw_ref, cols) + cbias_ref[0:1, cols])
        xh_s[:, cols] = x[CHUNK - 8:CHUNK, :]

    dtv = dtr_ref[...] + dtb_ref[...]
    dt = _softplus(dtv)
    if n_pad:
        dt = jnp.where(rowi >= n_pad, dt, 0.0)
    dta = dt * (-jnp.exp(alog_ref[...]))
    tri = (rowi >= coli).astype(F32)
    a_cs = jnp.dot(tri, dta, precision=lax.Precision.HIGHEST, preferred_element_type=F32)
    at_s[...] = a_cs.T
    stack = jnp.concatenate(
        [dt, jnp.exp(a_cs), jnp.exp(a_cs[CHUNK - 1:CHUNK, :] - a_cs)], axis=0)
    hi, lo = _split_hi_lo(stack)
    exp_s[...] = jnp.dot(jnp.concatenate([hi, lo], axis=1), e2_ref[...],
                         preferred_element_type=F32)

    for g in range(GROUPS):
        bg = xc_s[:, d + g * STATE:d + (g + 1) * STATE]
        cg = xc_s[:, d + d_gn + g * STATE:d + d_gn + (g + 1) * STATE]
        cbm_s[g] = lax.dot_general(cg.astype(BF16), bg.astype(BF16),
                                   (((1,), (1,)), ((), ())), preferred_element_type=F32)
        bt_s[g] = bg.T.astype(BF16)

    outproj_piece(3)

    for t in range(d // TW):
        cols = slice(t * TW, (t + 1) * TW)
        xdt = xc_s[:, cols] * exp_s[0:CHUNK, cols]
        xdt_s[:, cols] = xdt.astype(BF16)
        xd_s[:, cols] = (xdt * exp_s[2 * CHUNK:3 * CHUNK, cols]).astype(BF16)

    cdec = exp_s[2 * CHUNK - 1:2 * CHUNK, :]
    for g in range(GROUPS):
        cols = slice(g * gw, (g + 1) * gw)
        s_prev = state_s[:, cols]
        cg = xc_s[:, d + d_gn + g * STATE:d + d_gn + (g + 1) * STATE].astype(BF16)
        yoff = jnp.dot(cg, s_prev.astype(BF16), preferred_element_type=F32)
        ypre_s[:, cols] = yoff * exp_s[CHUNK:2 * CHUNK, cols]
        state_s[:, cols] = s_prev * cdec[:, cols] + jnp.dot(
            bt_s[g], xd_s[:, cols], preferred_element_type=F32)

    causal = rowi >= coli
    lane_lo = coli < HEADDIM
    ss = jnp.zeros((CHUNK, LANES), F32)
    for pr in range(heads // 2):
        if pr % (heads // 4) == 0:
            outproj_piece(4 + pr // (heads // 4))
        g = (2 * pr * HEADDIM) // gw
        ms_ = []
        for hh in (2 * pr, 2 * pr + 1):
            rowb = jnp.broadcast_to(at_s[hh:hh + 1, :], (CHUNK, CHUNK))
            seg = rowb.T - rowb
            decay = jnp.exp(jnp.where(causal, seg, -jnp.inf))
            ms_.append((cbm_s[g] * decay).astype(BF16))
        lhs = jnp.concatenate(ms_, axis=1)
        cols = slice(pr * LANES, (pr + 1) * LANES)
        xp = xdt_s[:, cols]
        zero = jnp.zeros_like(xp)
        rhs = jnp.concatenate([jnp.where(lane_lo, xp, zero), jnp.where(lane_lo, zero, xp)], axis=0)
        y = jnp.dot(lhs, rhs, preferred_element_type=F32)
        y = y + ypre_s[:, cols] + dexp_ref[0:1, cols] * xc_s[:, cols]
        y = y * _silu(_unpack(zs_ref[:, cols]).astype(F32))
        ytmp_s[:, cols] = y
        ss = ss + y * y
    rs = lax.rsqrt(jnp.sum(ss, axis=-1, keepdims=True) / d + EPS)
    for t in range(d // TW):
        cols = slice(t * TW, (t + 1) * TW)
        ymix_s[:, cols] = (ytmp_s[:, cols] * rs * nssd_ref[0:1, cols]).astype(BF16)

    ss = jnp.zeros((CHUNK, TW), F32)
    for t in range(d // TW):
        if t % 4 == 0:
            outproj_piece(6 + t // 4)
        cols = slice(t * TW, (t + 1) * TW)
        v = _unpack(c_ref[:, cols]).astype(F32) * _unpack(h_ref[:, cols]).astype(F32)
        ext = jnp.concatenate([vh_s[:, cols], v], axis=0)
        ysc = _unpack(b_ref[:, cols]).astype(F32) * _conv3(ext, scw_ref, cols)
        ysc = ysc * _silu(_unpack(zc_ref[:, cols]).astype(F32))
        ytmp_s[:, cols] = ysc
        vh_s[:, cols] = v[CHUNK - 8:CHUNK, :]
        ss = ss + ysc * ysc
    rs = lax.rsqrt(jnp.sum(ss, axis=-1, keepdims=True) / d + EPS)
    for t in range(d // TW):
        cols = slice(t * TW, (t + 1) * TW)
        ymix_s[:, d + t * TW:d + (t + 1) * TW] = (
            ytmp_s[:, cols] * rs * nsc_ref[0:1, cols]).astype(BF16)

    assert n_pieces == 8
    if with_outproj:
        rs = lax.rsqrt(jnp.sum(usq[0], axis=-1, keepdims=True) / d + EPS)
        for t in range(d // TW):
            cols = slice(t * TW, (t + 1) * TW)
            y_ref[:, cols] = y_ref[:, cols] * rs * nfin_ref[0:1, cols]
    else:
        y_ref[...] = jnp.zeros_like(y_ref)

    @pl.when(jnp.logical_and(c == nc - 1, s < total))
    def _fin():
        stT_ref[0] = state_s[...]
        for t in range(d // LANES):
            st_ref[0, t * LANES:(t + 1) * LANES, :] = state_s[:, t * LANES:(t + 1) * LANES].T
        xt_ref[0] = xh_s[...]
        vt_ref[0] = vh_s[...]


def _const_spec(shape):
    nd = len(shape)
    return pl.BlockSpec(shape, lambda s: (0,) * nd)


def _mixer(p32, dtr, u, s0, xt0, vt0, wts, *, nb, nc, row_block0, n_pad, with_outproj, name):
    d = u.shape[1]
    d_gn = GROUPS * STATE
    d_xbc = d + 2 * d_gn
    heads = d // HEADDIM
    total = nb * nc
    (cw, cbias, dtb, alog, dexp, nssd, scw, nsc, wout32, nfin, e2) = wts

    def rowmap(colblk):
        return lambda s: (row_block0 + jnp.minimum(s, total - 1), colblk)

    prev = lambda s: (jnp.maximum(s - 1, 0), 0)
    bmap = lambda s: (jnp.minimum(s // nc, nb - 1), 0, 0)

    seg = lambda k: pl.BlockSpec((CHUNK // 2, d), rowmap(k))
    in_specs = [
        seg(0), seg(1), seg(2), seg(3), seg(4), seg(5),
        pl.BlockSpec((CHUNK // 2, 2 * d_gn), rowmap((6 * d) // (2 * d_gn))),
        pl.BlockSpec((CHUNK, LANES), rowmap(0)),
        pl.BlockSpec((CHUNK, d), prev),
        _const_spec((STATE, d)), _const_spec((8, d_xbc)), _const_spec((8, d)),
        _const_spec(cw.shape), _const_spec(cbias.shape), _const_spec(dtb.shape),
        _const_spec(alog.shape), _const_spec(dexp.shape), _const_spec(nssd.shape),
        _const_spec(scw.shape), _const_spec(nsc.shape),
        pl.BlockSpec(wout32.shape, lambda s: (0, 0), pipeline_mode=pl.Buffered(1)),
        _const_spec(nfin.shape),
        pl.BlockSpec(e2.shape, lambda s: (0, 0), pipeline_mode=pl.Buffered(1)),
    ]
    out_specs = [
        pl.BlockSpec((CHUNK, d), prev),
        pl.BlockSpec((1, d, STATE), bmap),
        pl.BlockSpec((1, STATE, d), bmap),
        pl.BlockSpec((1, 8, d_xbc), bmap),
        pl.BlockSpec((1, 8, d), bmap),
    ]
    out_shape = [
        jax.ShapeDtypeStruct((total * CHUNK, d), F32),
        jax.ShapeDtypeStruct((nb, d, STATE), F32),
        jax.ShapeDtypeStruct((nb, STATE, d), F32),
        jax.ShapeDtypeStruct((nb, 8, d_xbc), F32),
        jax.ShapeDtypeStruct((nb, 8, d), F32),
    ]
    scratch = [
        pltpu.VMEM((STATE, d), F32),
        pltpu.VMEM((8, d_xbc), F32),
        pltpu.VMEM((8, d), F32),
        pltpu.VMEM((CHUNK, d_xbc), F32),
        pltpu.VMEM((3 * CHUNK, d), F32),
        pltpu.VMEM((GROUPS, CHUNK, CHUNK), F32),
        pltpu.VMEM((GROUPS, STATE, CHUNK), BF16),
        pltpu.VMEM((LANES, CHUNK), F32),
        pltpu.VMEM((CHUNK, d), BF16),
        pltpu.VMEM((CHUNK, d), BF16),
        pltpu.VMEM((CHUNK, d), F32),
        pltpu.VMEM((CHUNK, d), F32),
        pltpu.VMEM((CHUNK, 2 * d), BF16),
        pltpu.VMEM((CHUNK, 2 * d), BF16),
    ]
    return pl.pallas_call(
        functools.partial(_mixer_body, n_pad=n_pad, d=d, heads=heads, nc=nc, total=total,
                          with_outproj=with_outproj),
        grid=(total + 1 if with_outproj else total,),
        in_specs=in_specs,
        out_specs=out_specs,
        out_shape=out_shape,
        scratch_shapes=scratch,
        compiler_params=pltpu.CompilerParams(
            dimension_semantics=("arbitrary",),
            vmem_limit_bytes=VMEM_LIMIT_MIXER),
        name=name,
    )(p32, p32, p32, p32, p32, p32, p32, dtr, u, s0, xt0, vt0, cw, cbias, dtb, alog, dexp, nssd,
      scw, nsc, wout32, nfin, e2)


def _sample_front_body(zs_ref, zc_ref, b_ref, c_ref, h_ref, xr_ref, bcr_ref, dtr_ref,
                       cst_ref, sst_ref,
                       cw_ref, cbias_ref, dtb_ref, alog_ref, dexp_ref, scw_ref, nsc_ref, e2_ref,
                       cst_o, sst_o, xthl_o, bmat_o, cmat_o, dec_o, decx_o, xsd_o, gate_o, ysc_o,
                       xc_s, ytmp_s, *, d):
    d_gn = GROUPS * STATE
    d_xbc = d + 2 * d_gn
    nbt = cst_ref.shape[1]

    for t in range(d_xbc // TW):
        c0 = t * TW
        cols = slice(c0, c0 + TW)
        if c0 < d:
            x = _unpack(xr_ref[:, cols]).astype(F32)
        else:
            x = _unpack(bcr_ref[:, c0 - d:c0 - d + TW]).astype(F32)
        x0 = cst_ref[0, :, cols]
        x1 = cst_ref[1, :, cols]
        x2 = cst_ref[2, :, cols]
        acc = (x0 * cw_ref[0:1, cols] + x1 * cw_ref[1:2, cols] + x2 * cw_ref[2:3, cols]
               + x * cw_ref[3:4, cols] + cbias_ref[0:1, cols])
        xc_s[:, cols] = _silu(acc)
        cst_o[0, :, cols] = x1
        cst_o[1, :, cols] = x2
        cst_o[2, :, cols] = x

    dt = _softplus(dtr_ref[...] + dtb_ref[...])
    dec = jnp.exp(dt * (-jnp.exp(alog_ref[...])))
    dec_o[...] = dec
    hi, lo = _split_hi_lo(jnp.concatenate([dt, dec], axis=0))
    both = jnp.dot(jnp.concatenate([hi, lo], axis=1), e2_ref[...], preferred_element_type=F32)
    dt_exp = both[0:nbt]
    decx_o[...] = both[nbt:2 * nbt]

    bmat_o[...] = xc_s[:, d:d + d_gn].astype(BF16)
    cmat_o[...] = xc_s[:, d + d_gn:d + 2 * d_gn]
    bc = [jnp.sum(xc_s[:, d + g * STATE:d + (g + 1) * STATE]
                  * xc_s[:, d + d_gn + g * STATE:d + d_gn + (g + 1) * STATE], axis=-1, keepdims=True)
          for g in range(GROUPS)]

    for t in range(d // LANES):
        cols = slice(t * LANES, (t + 1) * LANES)
        xs = xc_s[:, cols]
        xdt = xs * dt_exp[:, cols]
        xsd_o[:, cols] = xs * dexp_ref[0:1, cols] + xdt * bc[(t * LANES) // (d // GROUPS)]
        gate_o[:, cols] = _silu(_unpack(zs_ref[:, cols]).astype(F32))
        xdt_t = xdt.T
        hi, lo = _split_hi_lo(xdt_t)
        xthl_o[cols, 0:nbt] = hi
        xthl_o[cols, nbt:2 * nbt] = lo

    ss = jnp.zeros((nbt, TW), F32)
    for t in range(d // TW):
        c0 = t * TW
        cols = slice(c0, c0 + TW)
        v = _unpack(c_ref[:, cols]).astype(F32) * _unpack(h_ref[:, cols]).astype(F32)
        v0 = sst_ref[:, c0:c0 + TW]
        v1 = sst_ref[:, d + c0:d + c0 + TW]
        acc = v0 * scw_ref[0:1, cols] + v1 * scw_ref[1:2, cols] + v * scw_ref[2:3, cols]
        ysc = _unpack(b_ref[:, cols]).astype(F32) * acc
        ysc = ysc * _silu(_unpack(zc_ref[:, cols]).astype(F32))
        ytmp_s[:, cols] = ysc
        sst_o[:, c0:c0 + TW] = v1
        sst_o[:, d + c0:d + c0 + TW] = v
        ss = ss + ysc * ysc
    rs = lax.rsqrt(jnp.sum(ss, axis=-1, keepdims=True) / d + EPS)
    for t in range(d // TW):
        cols = slice(t * TW, (t + 1) * TW)
        ysc_o[:, cols] = (ytmp_s[:, cols] * rs * nsc_ref[0:1, cols]).astype(BF16)


def _sample_front(p32, dtr, cst, sst, wts, *, row_block, d, name):
    d_gn = GROUPS * STATE
    d_xbc = d + 2 * d_gn
    nbt = cst.shape[1]
    (cw, cbias, dtb, alog, dexp, scw, nsc, e2) = wts
    seg = lambda k: pl.BlockSpec((nbt // 2, d), lambda i: (row_block, k))
    full = lambda a: pl.BlockSpec(a.shape, lambda i: (0,) * a.ndim)
    in_specs = [
        seg(0), seg(1), seg(2), seg(3), seg(4), seg(5),
        pl.BlockSpec((nbt // 2, 2 * d_gn), lambda i: (row_block, (6 * d) // (2 * d_gn))),
        pl.BlockSpec((nbt, LANES), lambda i: (row_block, 0)),
        full(cst), full(sst),
        full(cw), full(cbias), full(dtb), full(alog), full(dexp), full(scw), full(nsc), full(e2),
    ]
    out_shape = [
        jax.ShapeDtypeStruct(cst.shape, F32),
        jax.ShapeDtypeStruct(sst.shape, F32),
        jax.ShapeDtypeStruct((d, 2 * nbt), BF16),
        jax.ShapeDtypeStruct((nbt, d_gn), BF16),
        jax.ShapeDtypeStruct((nbt, d_gn), F32),
        jax.ShapeDtypeStruct((nbt, LANES), F32),
        jax.ShapeDtypeStruct((nbt, d), F32),
        jax.ShapeDtypeStruct((nbt, d), F32),
        jax.ShapeDtypeStruct((nbt, d), F32),
        jax.ShapeDtypeStruct((nbt, d), BF16),
    ]
    out_specs = [pl.BlockSpec(s.shape, lambda i, nd=len(s.shape): (0,) * nd) for s in out_shape]
    return pl.pallas_call(
        functools.partial(_sample_front_body, d=d),
        grid=(1,),
        in_specs=in_specs,
        out_specs=out_specs,
        out_shape=out_shape,
        scratch_shapes=[pltpu.VMEM((nbt, d_xbc), F32), pltpu.VMEM((nbt, d), F32)],
        compiler_params=pltpu.CompilerParams(vmem_limit_bytes=VMEM_LIMIT_SMALL),
        name=name,
    )(p32, p32, p32, p32, p32, p32, p32, dtr, cst, sst, cw, cbias, dtb, alog, dexp, scw, nsc, e2)


def _sample_update_body(dec_ref, st_ref, xthl_ref, bmat_ref, cmat_ref, so_ref, yraw_ref, *, heads):
    i = pl.program_id(0)
    nbt = bmat_ref.shape[0]
    d = xthl_ref.shape[0]
    gw = d // GROUPS
    hpg = heads // GROUPS
    rb = 16

    @pl.when(i == 0)
    def _():
        yraw_ref[...] = jnp.zeros_like(yraw_ref)

    rowi = lax.broadcasted_iota(jnp.int32, (nbt, STATE), 0)
    b16 = pl.multiple_of(((i * SEQ_PER_STEP) // rb) * rb, rb)
    row16 = lax.broadcasted_iota(jnp.int32, (rb, STATE), 0) + b16
    for g in range(GROUPS):
        bm = bmat_ref[:, g * STATE:(g + 1) * STATE]
        cm = cmat_ref[pl.ds(b16, rb), g * STATE:(g + 1) * STATE]
        yg = jnp.zeros((rb, gw), F32)
        for q in range(SEQ_PER_STEP):
            b = i * SEQ_PER_STEP + q
            r = jnp.where(rowi == b, bm, jnp.zeros_like(bm))
            upd = jnp.dot(xthl_ref[g * gw:(g + 1) * gw, :], jnp.concatenate([r, r], axis=0),
                          preferred_element_type=F32)
            s_old = st_ref[q, g * gw:(g + 1) * gw, :]
            parts = []
            for hh in range(hpg):
                parts.append(s_old[hh * HEADDIM:(hh + 1) * HEADDIM]
                             * dec_ref[b * heads + g * hpg + hh])
            so_ref[q, g * gw:(g + 1) * gw, :] = jnp.concatenate(parts, axis=0) + upd
            c16 = jnp.where(row16 == b, cm, 0.0)
            yg = yg + lax.dot_general(c16.astype(BF16), s_old.astype(BF16),
                                      (((1,), (1,)), ((), ())), preferred_element_type=F32)
        yraw_ref[pl.ds(b16, rb), g * gw:(g + 1) * gw] += yg


SEQ_PER_STEP = 2


def _sample_update(dec_flat, state, xthl, bmat, cmat, *, heads, name):
    nbt, d, n = state.shape
    assert nbt % SEQ_PER_STEP == 0 and 16 % SEQ_PER_STEP == 0
    return pl.pallas_call(
        functools.partial(_sample_update_body, heads=heads),
        grid=(nbt // SEQ_PER_STEP,),
        in_specs=[
            pl.BlockSpec(memory_space=pltpu.SMEM),
            pl.BlockSpec((SEQ_PER_STEP, d, n), lambda b: (b, 0, 0)),
            pl.BlockSpec(xthl.shape, lambda b: (0, 0)),
            pl.BlockSpec(bmat.shape, lambda b: (0, 0)),
            pl.BlockSpec(cmat.shape, lambda b: (0, 0)),
        ],
        out_specs=[
            pl.BlockSpec((SEQ_PER_STEP, d, n), lambda b: (b, 0, 0)),
            pl.BlockSpec((nbt, d), lambda b: (0, 0)),
        ],
        out_shape=[
            jax.ShapeDtypeStruct(state.shape, F32),
            jax.ShapeDtypeStruct((nbt, d), F32),
        ],
        compiler_params=pltpu.CompilerParams(
            dimension_semantics=("arbitrary",), vmem_limit_bytes=VMEM_LIMIT_SMALL),
        name=name,
    )(dec_flat, state, xthl, bmat, cmat)


def _sample_back_body(yraw_ref, decx_ref, xsd_ref, gate_ref, ysc_ref, u_ref, nssd_ref, wout_ref,
                      nfin_ref, y_ref, ymix_s, *, d):
    y = (decx_ref[...] * yraw_ref[...] + xsd_ref[...]) * gate_ref[...]
    rs = lax.rsqrt(jnp.mean(y * y, axis=-1, keepdims=True) + EPS)
    ymix_s[:, 0:d] = (y * rs * nssd_ref[...]).astype(BF16)
    ymix_s[:, d:2 * d] = ysc_ref[...]
    out = jnp.dot(ymix_s[...], _unpack(wout_ref[...]), preferred_element_type=F32)
    u = u_ref[...] + out
    ms = jnp.mean(u * u, axis=-1, keepdims=True)
    y_ref[...] = u * lax.rsqrt(ms + EPS) * nfin_ref[...]


def _sample_back(yraw, decx, xsd, gate, ysc, u, nssd, wout32, nfin, *, name):
    nbt, d = u.shape
    args = (yraw, decx, xsd, gate, ysc, u, nssd, wout32, nfin)
    return pl.pallas_call(
        functools.partial(_sample_back_body, d=d),
        grid=(1,),
        in_specs=[pl.BlockSpec(a.shape, lambda i: (0,) * a.ndim) for a in args],
        out_specs=pl.BlockSpec((nbt, d), lambda i: (0, 0)),
        out_shape=jax.ShapeDtypeStruct((nbt, d), F32),
        scratch_shapes=[pltpu.VMEM((nbt, 2 * d), BF16)],
        compiler_params=pltpu.CompilerParams(vmem_limit_bytes=VMEM_LIMIT_SMALL),
        name=name,
    )(*args)


def kernel(x_prompt, x_sample, state_ssm, state_ssd_conv, state_short_conv, meta_tokens, norm_w, w_in,
           conv_ssd_w, conv_ssd_b, dt_bias, a_log, d_skip, ssd_norm_w, conv_sc_w, sc_norm_w, w_out,
           final_norm_w):
    nb, seq, d = x_prompt.shape
    nbt = x_sample.shape[0]
    depth = norm_w.shape[0]
    assert depth == 1 and x_sample.shape[1] == 1
    heads = d // HEADDIM
    d_gn = GROUPS * STATE
    d_xbc = d + 2 * d_gn
    n_meta = meta_tokens.shape[0]
    assert seq % CHUNK == 0 and nbt == CHUNK and n_meta <= CHUNK and heads <= LANES
    nc = seq // CHUNK
    rows_p = nb * seq
    rows = rows_p + CHUNK + nbt

    o_dt = d + d_xbc
    o_sc = o_dt + heads
    assert d % TN == 0 and d_xbc % TN == 0 and (2 * d_gn) == TN and heads == W_ROW_UNIT
    row_starts = ([t * TN for t in range(d // TN)]
                  + [o_sc + t * TN for t in range(4 * d // TN)]
                  + [d + t * TN for t in range(d_xbc // TN)])

    w_t = w_in[0].T
    nw = norm_w[0][None, :]
    cw = conv_ssd_w[0]
    cbias = conv_ssd_b[0][None, :]
    dtb = jnp.pad(dt_bias[0], (0, LANES - heads))[None, :]
    alog = jnp.pad(a_log[0], (0, LANES - heads))[None, :]
    dexp = jnp.repeat(d_skip[0], HEADDIM)[None, :]
    nssd = ssd_norm_w[0][None, :]
    scw = conv_sc_w[0]
    nsc = sc_norm_w[0][None, :]
    nfin = final_norm_w[None, :]
    k_i = lax.broadcasted_iota(jnp.int32, (2 * LANES, d), 0)
    c_i = lax.broadcasted_iota(jnp.int32, (2 * LANES, d), 1)
    e2 = ((k_i % LANES) == (c_i // HEADDIM)).astype(BF16)

    wout32 = _pack_rows(w_out[0], rows_per_step=512, name="pack_wout")

    xp = x_prompt.reshape(rows_p, d)
    x_small = jnp.concatenate(
        [jnp.zeros((CHUNK - n_meta, d), F32), meta_tokens.astype(F32), x_sample[:, 0, :]], axis=0)
    hn32, dt_raw = _norm(xp, x_small, nw, w_t, dt_row0=o_dt, heads=heads, name="norm")
    tm = rows // 8
    assert rows % 8 == 0 and tm % 16 == 0
    p32 = _inproj(hn32, w_t, tm=tm, row_starts=row_starts, name="inproj")

    mixer_wts = (cw, cbias, dtb, alog, dexp, nssd, scw, nsc, wout32, nfin, e2)
    meta_block = rows_p // CHUNK
    sample_block = meta_block + 1

    zeros_state = jnp.zeros((STATE, d), F32)
    _, _, st_t_meta, xt_meta, vt_meta = _mixer(
        p32, dt_raw, x_small, zeros_state, jnp.zeros((8, d_xbc), F32), jnp.zeros((8, d), F32),
        mixer_wts[:8] + (jnp.zeros((8, LANES), U32),) + mixer_wts[9:],
        nb=1, nc=1, row_block0=meta_block, n_pad=CHUNK - n_meta, with_outproj=False, name="mixer_meta")

    y_p, st_p, _, xt_p, vt_p = _mixer(
        p32, dt_raw, xp, st_t_meta[0], xt_meta[0], vt_meta[0],
        mixer_wts, nb=nb, nc=nc, row_block0=0, n_pad=0, with_outproj=True, name="mixer_prompt")

    cst = jnp.transpose(state_ssd_conv[0], (1, 0, 2))
    sst = state_short_conv[0].reshape(nbt, 2 * d)
    (cst_n, sst_n, xthl, bmat, cmat, dec, decx, xsd, gate, ysc) = _sample_front(
        p32, dt_raw, cst, sst, (cw, cbias, dtb, alog, dexp, scw, nsc, e2),
        row_block=sample_block, d=d, name="sample_front")
    st_s, yraw = _sample_update(
        dec[:, :heads].reshape(-1), state_ssm[0].reshape(nbt, d, STATE), xthl, bmat, cmat,
        heads=heads, name="sample_update")
    y_s = _sample_back(yraw, decx, xsd, gate, ysc, x_sample[:, 0, :], nssd, wout32, nfin,
                       name="sample_back")

    return (
        y_p.reshape(nb, seq, d),
        y_s.reshape(nbt, 1, d),
        st_p.reshape(1, nb, heads, HEADDIM, STATE),
        xt_p[:, 5:8, :][None],
        vt_p[:, 6:8, :][None],
        st_s.reshape(1, nbt, heads, HEADDIM, STATE),
        jnp.transpose(cst_n, (1, 0, 2))[None],
        sst_n.reshape(1, nbt, 2, d),
    )
```

```python
import functools

import jax
import jax.numpy as jnp
from jax import lax
from jax.experimental import pallas as pl
from jax.experimental.pallas import tpu as pltpu

F32 = jnp.float32
BF16 = jnp.bfloat16
U32 = jnp.uint32
EPS = 1e-5

HEADDIM = 64
GROUPS = 4
STATE = 128
CHUNK = 128
LANES = 128
TW = 256
TN = 1024
NORM_ROWS = 256

VMEM_LIMIT_INPROJ = 52 * 1024 * 1024
VMEM_LIMIT_MIXER = 60 * 1024 * 1024
VMEM_LIMIT_SMALL = 48 * 1024 * 1024


def _silu(x):
    h = 0.5 * x
    return h + h * jnp.tanh(h)


def _softplus(x):
    return jnp.maximum(x, 0.0) + jnp.log1p(jnp.exp(-jnp.abs(x)))


def _split_hi_lo(x):
    hi = x.astype(BF16)
    lo = (x - hi.astype(F32)).astype(BF16)
    return hi, lo


def _pack(x_bf16):
    return pltpu.bitcast(x_bf16, U32)


def _unpack(x_u32):
    return pltpu.bitcast(x_u32, BF16)


def _pack_rows_body(w_ref, o_ref):
    o_ref[...] = _pack(w_ref[...].astype(BF16))


def _pack_rows(w, *, rows_per_step, name):
    k, n = w.shape
    return pl.pallas_call(
        _pack_rows_body,
        grid=(k // rows_per_step,),
        in_specs=[pl.BlockSpec((rows_per_step, n), lambda i: (i, 0))],
        out_specs=pl.BlockSpec((rows_per_step // 2, n), lambda i: (i, 0)),
        out_shape=jax.ShapeDtypeStruct((k // 2, n), U32),
        compiler_params=pltpu.CompilerParams(dimension_semantics=("arbitrary",)),
        name=name,
    )(w)


def _norm_body(xp_ref, xs_ref, nw_ref, wdt_ref, hn_ref, dt_ref, wdt_s, *, n_prompt_steps, heads):
    i = pl.program_id(0)
    rg = 64

    @pl.when(i == 0)
    def _():
        wdt_s[...] = jnp.zeros_like(wdt_s)
        wdt_s[0:heads, :] = wdt_ref[...].astype(BF16)

    def norm_from(src_ref):
        nw = nw_ref[...]
        for r in range(NORM_ROWS // rg):
            x = src_ref[r * rg:(r + 1) * rg, :]
            ms = jnp.mean(x * x, axis=-1, keepdims=True)
            hn = (x * lax.rsqrt(ms + EPS) * nw).astype(BF16)
            hn_ref[r * (rg // 2):(r + 1) * (rg // 2), :] = _pack(hn)

    @pl.when(i < n_prompt_steps)
    def _():
        norm_from(xp_ref)

    @pl.when(i >= n_prompt_steps)
    def _():
        norm_from(xs_ref)

    dt_ref[...] = lax.dot_general(_unpack(hn_ref[...]), wdt_s[...], (((1,), (1,)), ((), ())),
                                  preferred_element_type=F32)


def _norm(xp, xs, nw, w_t, *, dt_row0, heads, name):
    rows_p, d = xp.shape
    rows_s = xs.shape[0]
    assert rows_p % NORM_ROWS == 0 and rows_s == NORM_ROWS and dt_row0 % heads == 0
    n_prompt_steps = rows_p // NORM_ROWS
    rows = rows_p + rows_s
    return pl.pallas_call(
        functools.partial(_norm_body, n_prompt_steps=n_prompt_steps, heads=heads),
        grid=(n_prompt_steps + 1,),
        in_specs=[
            pl.BlockSpec((NORM_ROWS, d), lambda i: (jnp.minimum(i, n_prompt_steps - 1), 0)),
            pl.BlockSpec((NORM_ROWS, d), lambda i: (0, 0)),
            pl.BlockSpec((1, d), lambda i: (0, 0)),
            pl.BlockSpec((heads, d), lambda i: (dt_row0 // heads, 0)),
        ],
        out_specs=[
            pl.BlockSpec((NORM_ROWS // 2, d), lambda i: (i, 0)),
            pl.BlockSpec((NORM_ROWS, LANES), lambda i: (i, 0)),
        ],
        out_shape=[
            jax.ShapeDtypeStruct((rows // 2, d), U32),
            jax.ShapeDtypeStruct((rows, LANES), F32),
        ],
        scratch_shapes=[pltpu.VMEM((LANES, d), BF16)],
        compiler_params=pltpu.CompilerParams(dimension_semantics=("arbitrary",)),
        name=name,
    )(xp, xs, nw, w_t)


W_ROW_UNIT = 32


def _inproj_body(hn_ref, w_ref, p_ref, wbf_s):
    rg = 64

    @pl.when(pl.program_id(1) == 0)
    def _():
        def body(r, carry):
            sl = pl.ds(pl.multiple_of(r * rg, rg), rg)
            wbf_s[sl, :] = w_ref[sl, :].astype(BF16)
            return carry

        lax.fori_loop(0, TN // rg, body, 0)

    acc = lax.dot_general(_unpack(hn_ref[...]), wbf_s[...], (((1,), (1,)), ((), ())),
                          preferred_element_type=F32)
    p_ref[...] = _pack(acc.astype(BF16))


def _inproj(hn32, w_t, *, tm, row_starts, name):
    rows2, d = hn32.shape
    nj = len(row_starts)
    ni = (2 * rows2) // tm
    assert all(r % W_ROW_UNIT == 0 for r in row_starts)
    starts = jnp.asarray([r // W_ROW_UNIT for r in row_starts], jnp.int32)

    grid_spec = pltpu.PrefetchScalarGridSpec(
        num_scalar_prefetch=1,
        grid=(nj, ni),
        in_specs=[
            pl.BlockSpec((tm // 2, d), lambda j, i, st: (i, 0)),
            pl.BlockSpec((pl.Element(TN), pl.Element(d)), lambda j, i, st: (st[j] * W_ROW_UNIT, 0)),
        ],
        out_specs=pl.BlockSpec((tm // 2, TN), lambda j, i, st: (i, j)),
        scratch_shapes=[pltpu.VMEM((TN, d), BF16)],
    )

    def body(st_ref, *refs):
        _inproj_body(*refs)

    return pl.pallas_call(
        body,
        grid_spec=grid_spec,
        out_shape=jax.ShapeDtypeStruct((rows2, nj * TN), U32),
        compiler_params=pltpu.CompilerParams(
            dimension_semantics=("arbitrary", "arbitrary"),
            vmem_limit_bytes=VMEM_LIMIT_INPROJ),
        name=name,
    )(starts, hn32, w_t)


def _conv4(ext, w_ref, cols):
    s1 = pltpu.roll(ext, 1, 0)
    q = ext * w_ref[1:2, cols] + s1 * w_ref[0:1, cols]
    out = ext * w_ref[3:4, cols] + s1 * w_ref[2:3, cols] + pltpu.roll(q, 2, 0)
    return out[8:, :]


def _conv3(ext, w_ref, cols):
    s1 = pltpu.roll(ext, 1, 0)
    q = ext * w_ref[1:2, cols] + s1 * w_ref[0:1, cols]
    out = ext * w_ref[2:3, cols] + pltpu.roll(q, 1, 0)
    return out[8:, :]


def _sample_state_step(i, spp, rb, dec_ref, st_ref, xthl_ref, bmat_ref, cmat_ref, so_ref, yrow_ref,
                       *, heads):
    nbt = bmat_ref.shape[0]
    d = xthl_ref.shape[0]
    gw = d // GROUPS
    hpg = heads // GROUPS
    rowi = lax.broadcasted_iota(jnp.int32, (nbt, STATE), 0)
    b_rb = pl.multiple_of(((i * spp) // rb) * rb, rb)
    row_rb = lax.broadcasted_iota(jnp.int32, (rb, STATE), 0) + b_rb
    for g in range(GROUPS):
        bm = bmat_ref[:, g * STATE:(g + 1) * STATE]
        cm = cmat_ref[pl.ds(b_rb, rb), g * STATE:(g + 1) * STATE]
        yg = jnp.zeros((rb, gw), F32)
        for q in range(spp):
            b = i * spp + q
            r = jnp.where(rowi == b, bm, jnp.zeros_like(bm))
            upd = jnp.dot(xthl_ref[g * gw:(g + 1) * gw, :], jnp.concatenate([r, r], axis=0),
                          preferred_element_type=F32)
            s_old = st_ref[q, g * gw:(g + 1) * gw, :]
            parts = []
            for hh in range(hpg):
                parts.append(s_old[hh * HEADDIM:(hh + 1) * HEADDIM]
                             * dec_ref[b * heads + g * hpg + hh])
            so_ref[q, g * gw:(g + 1) * gw, :] = jnp.concatenate(parts, axis=0) + upd
            c_b = jnp.where(row_rb == b, cm, 0.0)
            yg = yg + lax.dot_general(c_b.astype(BF16), s_old.astype(BF16),
                                      (((1,), (1,)), ((), ())), preferred_element_type=F32)
        yrow_ref[0, :, g * gw:(g + 1) * gw] = yg


N_MIXER_IN = 23
N_MIXER_OUT = 5
N_SAMPLE_IN = 5
N_SAMPLE_OUT = 2


def _mixer_body(*refs, n_pad, d, heads, nc, total, with_outproj, sample):
    n_in = N_MIXER_IN + (N_SAMPLE_IN if sample else 0)
    n_out = N_MIXER_OUT + (N_SAMPLE_OUT if sample else 0)
    (zs_ref, zc_ref, b_ref, c_ref, h_ref, xr_ref, bcr_ref, dtr_ref, u_ref,
     s0_ref, xt0_ref, vt0_ref,
     cw_ref, cbias_ref, dtb_ref, alog_ref, dexp_ref, nssd_ref, scw_ref, nsc_ref,
     wout_ref, nfin_ref, e2_ref) = refs[:N_MIXER_IN]
    y_ref, st_ref, stT_ref, xt_ref, vt_ref = refs[n_in:n_in + N_MIXER_OUT]
    (state_s, xh_s, vh_s, xc_s, exp_s, cbm_s, bt_s, at_s, xdt_s, xd_s, ypre_s,
     ytmp_s, ymix_s, ymixp_s) = refs[n_in + n_out:]
    s = pl.program_id(0)
    c = s % nc
    d_gn = GROUPS * STATE
    d_xbc = d + 2 * d_gn
    gw = d // GROUPS

    @pl.when(s == 0)
    def _zero():
        ymix_s[...] = jnp.zeros_like(ymix_s)

    @pl.when(c == 0)
    def _init():
        state_s[...] = s0_ref[...]
        xh_s[...] = xt0_ref[...]
        vh_s[...] = vt0_ref[...]

    if with_outproj:
        ymixp_s[...] = ymix_s[...]
    n_pieces = d // TW
    usq = [jnp.zeros((CHUNK, TW), F32)]

    def outproj_piece(n):
        if not with_outproj:
            return
        cols = slice(n * TW, (n + 1) * TW)
        u = u_ref[:, cols] + jnp.dot(ymixp_s[...], _unpack(wout_ref[:, cols]),
                                     preferred_element_type=F32)
        y_ref[:, cols] = u
        usq[0] = usq[0] + u * u

    if sample:
        spp, rb, n_steps = sample
        _sample_state_step(jnp.minimum(pl.program_id(0), n_steps - 1), spp, rb,
                           *refs[N_MIXER_IN:n_in], *refs[n_in + N_MIXER_OUT:n_in + n_out],
                           heads=heads)

    rowi = lax.broadcasted_iota(jnp.int32, (CHUNK, LANES), 0)
    coli = lax.broadcasted_iota(jnp.int32, (CHUNK, LANES), 1)

    for t in range(d_xbc // TW):
        if t % 5 == 0:
            outproj_piece(t // 5)
        c0 = t * TW
        cols = slice(c0, c0 + TW)
        if c0 < d:
            x = _unpack(xr_ref[:, cols]).astype(F32)
        else:
            x = _unpack(bcr_ref[:, c0 - d:c0 - d + TW]).astype(F32)
        ext = jnp.concatenate([xh_s[:, cols], x], axis=0)
        xc_s[:, cols] = _silu(_conv4(ext, cw_ref, cols) + cbias_ref[0:1, cols])
        xh_s[:, cols] = x[CHUNK - 8:CHUNK, :]

    dtv = dtr_ref[...] + dtb_ref[...]
    dt = _softplus(dtv)
    if n_pad:
        dt = jnp.where(rowi >= n_pad, dt, 0.0)
    dta = dt * (-jnp.exp(alog_ref[...]))
    tri = (rowi >= coli).astype(F32)
    a_cs = jnp.dot(tri, dta, precision=lax.Precision.HIGHEST, preferred_element_type=F32)
    at_s[...] = a_cs.T
    stack = jnp.concatenate(
        [dt, jnp.exp(a_cs), jnp.exp(a_cs[CHUNK - 1:CHUNK, :] - a_cs)], axis=0)
    hi, lo = _split_hi_lo(stack)
    exp_s[...] = jnp.dot(jnp.concatenate([hi, lo], axis=1), e2_ref[...],
                         preferred_element_type=F32)

    for g in range(GROUPS):
        bg = xc_s[:, d + g * STATE:d + (g + 1) * STATE]
        cg = xc_s[:, d + d_gn + g * STATE:d + d_gn + (g + 1) * STATE]
        cbm_s[g] = lax.dot_general(cg.astype(BF16), bg.astype(BF16),
                                   (((1,), (1,)), ((), ())), preferred_element_type=F32)
        bt_s[g] = bg.T.astype(BF16)

    outproj_piece(3)

    for t in range(d // TW):
        cols = slice(t * TW, (t + 1) * TW)
        xdt = xc_s[:, cols] * exp_s[0:CHUNK, cols]
        xdt_s[:, cols] = xdt.astype(BF16)
        xd_s[:, cols] = (xdt * exp_s[2 * CHUNK:3 * CHUNK, cols]).astype(BF16)

    cdec = exp_s[2 * CHUNK - 1:2 * CHUNK, :]
    for g in range(GROUPS):
        cols = slice(g * gw, (g + 1) * gw)
        s_prev = state_s[:, cols]
        cg = xc_s[:, d + d_gn + g * STATE:d + d_gn + (g + 1) * STATE].astype(BF16)
        yoff = jnp.dot(cg, s_prev.astype(BF16), preferred_element_type=F32)
        ypre_s[:, cols] = yoff * exp_s[CHUNK:2 * CHUNK, cols]
        state_s[:, cols] = s_prev * cdec[:, cols] + jnp.dot(
            bt_s[g], xd_s[:, cols], preferred_element_type=F32)

    causal = rowi >= coli
    lane_lo = coli < HEADDIM
    ss = jnp.zeros((CHUNK, LANES), F32)
    for pr in range(heads // 2):
        if pr % (heads // 4) == 0:
            outproj_piece(4 + pr // (heads // 4))
        g = (2 * pr * HEADDIM) // gw
        ms_ = []
        for hh in (2 * pr, 2 * pr + 1):
            rowb = jnp.broadcast_to(at_s[hh:hh + 1, :], (CHUNK, CHUNK))
            seg = rowb.T - rowb
            decay = jnp.exp(jnp.where(causal, seg, -jnp.inf))
            ms_.append((cbm_s[g] * decay).astype(BF16))
        lhs = jnp.concatenate(ms_, axis=1)
        cols = slice(pr * LANES, (pr + 1) * LANES)
        xp = xdt_s[:, cols]
        zero = jnp.zeros_like(xp)
        rhs = jnp.concatenate([jnp.where(lane_lo, xp, zero), jnp.where(lane_lo, zero, xp)], axis=0)
        y = jnp.dot(lhs, rhs, preferred_element_type=F32)
        y = y + ypre_s[:, cols] + dexp_ref[0:1, cols] * xc_s[:, cols]
        y = y * _silu(_unpack(zs_ref[:, cols]).astype(F32))
        ytmp_s[:, cols] = y
        ss = ss + y * y
    rs = lax.rsqrt(jnp.sum(ss, axis=-1, keepdims=True) / d + EPS)
    for t in range(d // TW):
        cols = slice(t * TW, (t + 1) * TW)
        ymix_s[:, cols] = (ytmp_s[:, cols] * rs * nssd_ref[0:1, cols]).astype(BF16)

    ss = jnp.zeros((CHUNK, TW), F32)
    for t in range(d // TW):
        if t % 4 == 0:
            outproj_piece(6 + t // 4)
        cols = slice(t * TW, (t + 1) * TW)
        v = _unpack(c_ref[:, cols]).astype(F32) * _unpack(h_ref[:, cols]).astype(F32)
        ext = jnp.concatenate([vh_s[:, cols], v], axis=0)
        ysc = _unpack(b_ref[:, cols]).astype(F32) * _conv3(ext, scw_ref, cols)
        ysc = ysc * _silu(_unpack(zc_ref[:, cols]).astype(F32))
        ytmp_s[:, cols] = ysc
        vh_s[:, cols] = v[CHUNK - 8:CHUNK, :]
        ss = ss + ysc * ysc
    rs = lax.rsqrt(jnp.sum(ss, axis=-1, keepdims=True) / d + EPS)
    for t in range(d // TW):
        cols = slice(t * TW, (t + 1) * TW)
        ymix_s[:, d + t * TW:d + (t + 1) * TW] = (
            ytmp_s[:, cols] * rs * nsc_ref[0:1, cols]).astype(BF16)

    assert n_pieces == 8
    if with_outproj:
        rs = lax.rsqrt(jnp.sum(usq[0], axis=-1, keepdims=True) / d + EPS)
        for t in range(d // TW):
            cols = slice(t * TW, (t + 1) * TW)
            y_ref[:, cols] = y_ref[:, cols] * rs * nfin_ref[0:1, cols]
    else:
        y_ref[...] = jnp.zeros_like(y_ref)

    @pl.when(jnp.logical_and(c == nc - 1, s < total))
    def _fin():
        stT_ref[0] = state_s[...]
        for t in range(d // LANES):
            st_ref[0, t * LANES:(t + 1) * LANES, :] = state_s[:, t * LANES:(t + 1) * LANES].T
        xt_ref[0] = xh_s[...]
        vt_ref[0] = vh_s[...]


def _const_spec(shape):
    nd = len(shape)
    return pl.BlockSpec(shape, lambda s: (0,) * nd)


def _mixer(p32, dtr, u, s0, xt0, vt0, wts, *, nb, nc, row_block0, n_pad, with_outproj, name,
           sample_args=None):
    d = u.shape[1]
    d_gn = GROUPS * STATE
    d_xbc = d + 2 * d_gn
    heads = d // HEADDIM
    total = nb * nc
    (cw, cbias, dtb, alog, dexp, nssd, scw, nsc, wout32, nfin, e2) = wts

    def rowmap(colblk):
        return lambda s: (row_block0 + jnp.minimum(s, total - 1), colblk)

    prev = lambda s: (jnp.maximum(s - 1, 0), 0)
    bmap = lambda s: (jnp.minimum(s // nc, nb - 1), 0, 0)

    seg = lambda k: pl.BlockSpec((CHUNK // 2, d), rowmap(k))
    in_specs = [
        seg(0), seg(1), seg(2), seg(3), seg(4), seg(5),
        pl.BlockSpec((CHUNK // 2, 2 * d_gn), rowmap((6 * d) // (2 * d_gn))),
        pl.BlockSpec((CHUNK, LANES), rowmap(0)),
        pl.BlockSpec((CHUNK, d), prev),
        _const_spec((STATE, d)), _const_spec((8, d_xbc)), _const_spec((8, d)),
        _const_spec(cw.shape), _const_spec(cbias.shape), _const_spec(dtb.shape),
        _const_spec(alog.shape), _const_spec(dexp.shape), _const_spec(nssd.shape),
        _const_spec(scw.shape), _const_spec(nsc.shape),
        pl.BlockSpec(wout32.shape, lambda s: (0, 0), pipeline_mode=pl.Buffered(1)),
        _const_spec(nfin.shape),
        pl.BlockSpec(e2.shape, lambda s: (0, 0), pipeline_mode=pl.Buffered(1)),
    ]
    out_specs = [
        pl.BlockSpec((CHUNK, d), prev),
        pl.BlockSpec((1, d, STATE), bmap),
        pl.BlockSpec((1, STATE, d), bmap),
        pl.BlockSpec((1, 8, d_xbc), bmap),
        pl.BlockSpec((1, 8, d), bmap),
    ]
    out_shape = [
        jax.ShapeDtypeStruct((total * CHUNK, d), F32),
        jax.ShapeDtypeStruct((nb, d, STATE), F32),
        jax.ShapeDtypeStruct((nb, STATE, d), F32),
        jax.ShapeDtypeStruct((nb, 8, d_xbc), F32),
        jax.ShapeDtypeStruct((nb, 8, d), F32),
    ]
    args = [p32, p32, p32, p32, p32, p32, p32, dtr, u, s0, xt0, vt0, cw, cbias, dtb, alog, dexp, nssd,
            scw, nsc, wout32, nfin, e2]
    assert len(args) == N_MIXER_IN and len(in_specs) == N_MIXER_IN
    sample = None
    if sample_args is not None:
        dec_flat, state, xthl, bmat, cmat = sample_args
        nbt = state.shape[0]
        n_grid = total + 1 if with_outproj else total
        spp = 2
        while spp * n_grid < nbt:
            spp *= 2
        rb = max(16, spp)
        assert nbt % rb == 0
        n_steps = nbt // spp
        sample = (spp, rb, n_steps)
        smap = lambda s: (jnp.minimum(s, n_steps - 1), 0, 0)
        in_specs += [
            pl.BlockSpec(memory_space=pltpu.SMEM),
            pl.BlockSpec((spp, d, STATE), smap),
            pl.BlockSpec(xthl.shape, lambda s: (0, 0), pipeline_mode=pl.Buffered(1)),
            _const_spec(bmat.shape), _const_spec(cmat.shape),
        ]
        out_specs += [pl.BlockSpec((spp, d, STATE), smap), pl.BlockSpec((1, rb, d), smap)]
        out_shape += [jax.ShapeDtypeStruct(state.shape, F32),
                      jax.ShapeDtypeStruct((n_steps, rb, d), F32)]
        args += [dec_flat, state, xthl, bmat, cmat]
    scratch = [
        pltpu.VMEM((STATE, d), F32),
        pltpu.VMEM((8, d_xbc), F32),
        pltpu.VMEM((8, d), F32),
        pltpu.VMEM((CHUNK, d_xbc), F32),
        pltpu.VMEM((3 * CHUNK, d), F32),
        pltpu.VMEM((GROUPS, CHUNK, CHUNK), F32),
        pltpu.VMEM((GROUPS, STATE, CHUNK), BF16),
        pltpu.VMEM((LANES, CHUNK), F32),
        pltpu.VMEM((CHUNK, d), BF16),
        pltpu.VMEM((CHUNK, d), BF16),
        pltpu.VMEM((CHUNK, d), F32),
        pltpu.VMEM((CHUNK, d), F32),
        pltpu.VMEM((CHUNK, 2 * d), BF16),
        pltpu.VMEM((CHUNK, 2 * d), BF16),
    ]
    return pl.pallas_call(
        functools.partial(_mixer_body, n_pad=n_pad, d=d, heads=heads, nc=nc, total=total,
                          with_outproj=with_outproj, sample=sample),
        grid=(total + 1 if with_outproj else total,),
        in_specs=in_specs,
        out_specs=out_specs,
        out_shape=out_shape,
        scratch_shapes=scratch,
        compiler_params=pltpu.CompilerParams(
            dimension_semantics=("arbitrary",),
            vmem_limit_bytes=VMEM_LIMIT_MIXER),
        name=name,
    )(*args)


def _sample_front_body(zs_ref, zc_ref, b_ref, c_ref, h_ref, xr_ref, bcr_ref, dtr_ref,
                       cst_ref, sst_ref,
                       cw_ref, cbias_ref, dtb_ref, alog_ref, dexp_ref, scw_ref, nsc_ref, e2_ref,
                       cst_o, sst_o, xthl_o, bmat_o, cmat_o, dec_o, decx_o, xsd_o, gate_o, ysc_o,
                       xc_s, ytmp_s, *, d):
    d_gn = GROUPS * STATE
    d_xbc = d + 2 * d_gn
    nbt = cst_ref.shape[1]

    for t in range(d_xbc // TW):
        c0 = t * TW
        cols = slice(c0, c0 + TW)
        if c0 < d:
            x = _unpack(xr_ref[:, cols]).astype(F32)
        else:
            x = _unpack(bcr_ref[:, c0 - d:c0 - d + TW]).astype(F32)
        x0 = cst_ref[0, :, cols]
        x1 = cst_ref[1, :, cols]
        x2 = cst_ref[2, :, cols]
        acc = (x0 * cw_ref[0:1, cols] + x1 * cw_ref[1:2, cols] + x2 * cw_ref[2:3, cols]
               + x * cw_ref[3:4, cols] + cbias_ref[0:1, cols])
        xc_s[:, cols] = _silu(acc)
        cst_o[0, :, cols] = x1
        cst_o[1, :, cols] = x2
        cst_o[2, :, cols] = x

    dt = _softplus(dtr_ref[...] + dtb_ref[...])
    dec = jnp.exp(dt * (-jnp.exp(alog_ref[...])))
    dec_o[...] = dec
    hi, lo = _split_hi_lo(jnp.concatenate([dt, dec], axis=0))
    both = jnp.dot(jnp.concatenate([hi, lo], axis=1), e2_ref[...], preferred_element_type=F32)
    dt_exp = both[0:nbt]
    decx_o[...] = both[nbt:2 * nbt]

    bmat_o[...] = xc_s[:, d:d + d_gn].astype(BF16)
    cmat_o[...] = xc_s[:, d + d_gn:d + 2 * d_gn]
    bc = [jnp.sum(xc_s[:, d + g * STATE:d + (g + 1) * STATE]
                  * xc_s[:, d + d_gn + g * STATE:d + d_gn + (g + 1) * STATE], axis=-1, keepdims=True)
          for g in range(GROUPS)]

    for t in range(d // LANES):
        cols = slice(t * LANES, (t + 1) * LANES)
        xs = xc_s[:, cols]
        xdt = xs * dt_exp[:, cols]
        xsd_o[:, cols] = xs * dexp_ref[0:1, cols] + xdt * bc[(t * LANES) // (d // GROUPS)]
        gate_o[:, cols] = _silu(_unpack(zs_ref[:, cols]).astype(F32))
        xdt_t = xdt.T
        hi, lo = _split_hi_lo(xdt_t)
        xthl_o[cols, 0:nbt] = hi
        xthl_o[cols, nbt:2 * nbt] = lo

    ss = jnp.zeros((nbt, TW), F32)
    for t in range(d // TW):
        c0 = t * TW
        cols = slice(c0, c0 + TW)
        v = _unpack(c_ref[:, cols]).astype(F32) * _unpack(h_ref[:, cols]).astype(F32)
        v0 = sst_ref[:, c0:c0 + TW]
        v1 = sst_ref[:, d + c0:d + c0 + TW]
        acc = v0 * scw_ref[0:1, cols] + v1 * scw_ref[1:2, cols] + v * scw_ref[2:3, cols]
        ysc = _unpack(b_ref[:, cols]).astype(F32) * acc
        ysc = ysc * _silu(_unpack(zc_ref[:, cols]).astype(F32))
        ytmp_s[:, cols] = ysc
        sst_o[:, c0:c0 + TW] = v1
        sst_o[:, d + c0:d + c0 + TW] = v
        ss = ss + ysc * ysc
    rs = lax.rsqrt(jnp.sum(ss, axis=-1, keepdims=True) / d + EPS)
    for t in range(d // TW):
        cols = slice(t * TW, (t + 1) * TW)
        ysc_o[:, cols] = (ytmp_s[:, cols] * rs * nsc_ref[0:1, cols]).astype(BF16)


def _sample_front(p32, dtr, cst, sst, wts, *, row_block, d, name):
    d_gn = GROUPS * STATE
    d_xbc = d + 2 * d_gn
    nbt = cst.shape[1]
    (cw, cbias, dtb, alog, dexp, scw, nsc, e2) = wts
    seg = lambda k: pl.BlockSpec((nbt // 2, d), lambda i: (row_block, k))
    full = lambda a: pl.BlockSpec(a.shape, lambda i: (0,) * a.ndim)
    in_specs = [
        seg(0), seg(1), seg(2), seg(3), seg(4), seg(5),
        pl.BlockSpec((nbt // 2, 2 * d_gn), lambda i: (row_block, (6 * d) // (2 * d_gn))),
        pl.BlockSpec((nbt, LANES), lambda i: (row_block, 0)),
        full(cst), full(sst),
        full(cw), full(cbias), full(dtb), full(alog), full(dexp), full(scw), full(nsc), full(e2),
    ]
    out_shape = [
        jax.ShapeDtypeStruct(cst.shape, F32),
        jax.ShapeDtypeStruct(sst.shape, F32),
        jax.ShapeDtypeStruct((d, 2 * nbt), BF16),
        jax.ShapeDtypeStruct((nbt, d_gn), BF16),
        jax.ShapeDtypeStruct((nbt, d_gn), F32),
        jax.ShapeDtypeStruct((nbt, LANES), F32),
        jax.ShapeDtypeStruct((nbt, d), F32),
        jax.ShapeDtypeStruct((nbt, d), F32),
        jax.ShapeDtypeStruct((nbt, d), F32),
        jax.ShapeDtypeStruct((nbt, d), BF16),
    ]
    out_specs = [pl.BlockSpec(s.shape, lambda i, nd=len(s.shape): (0,) * nd) for s in out_shape]
    return pl.pallas_call(
        functools.partial(_sample_front_body, d=d),
        grid=(1,),
        in_specs=in_specs,
        out_specs=out_specs,
        out_shape=out_shape,
        scratch_shapes=[pltpu.VMEM((nbt, d_xbc), F32), pltpu.VMEM((nbt, d), F32)],
        compiler_params=pltpu.CompilerParams(vmem_limit_bytes=VMEM_LIMIT_SMALL),
        name=name,
    )(p32, p32, p32, p32, p32, p32, p32, dtr, cst, sst, cw, cbias, dtb, alog, dexp, scw, nsc, e2)


def _sample_back_body(yrow_ref, decx_ref, xsd_ref, gate_ref, ysc_ref, u_ref, nssd_ref, wout_ref,
                      nfin_ref, y_ref, ymix_s, yraw_s, *, d, steps_per_block):
    n_steps, rb, _ = yrow_ref.shape
    for blk in range(n_steps // steps_per_block):
        acc = yrow_ref[blk * steps_per_block]
        for k in range(1, steps_per_block):
            acc = acc + yrow_ref[blk * steps_per_block + k]
        yraw_s[blk * rb:(blk + 1) * rb, :] = acc
    y = (decx_ref[...] * yraw_s[...] + xsd_ref[...]) * gate_ref[...]
    rs = lax.rsqrt(jnp.mean(y * y, axis=-1, keepdims=True) + EPS)
    ymix_s[:, 0:d] = (y * rs * nssd_ref[...]).astype(BF16)
    ymix_s[:, d:2 * d] = ysc_ref[...]
    out = jnp.dot(ymix_s[...], _unpack(wout_ref[...]), preferred_element_type=F32)
    u = u_ref[...] + out
    ms = jnp.mean(u * u, axis=-1, keepdims=True)
    y_ref[...] = u * lax.rsqrt(ms + EPS) * nfin_ref[...]


def _sample_back(yrow, decx, xsd, gate, ysc, u, nssd, wout32, nfin, *, name):
    nbt, d = u.shape
    n_steps, rb, _ = yrow.shape
    args = (yrow, decx, xsd, gate, ysc, u, nssd, wout32, nfin)
    return pl.pallas_call(
        functools.partial(_sample_back_body, d=d, steps_per_block=(n_steps * rb) // nbt),
        grid=(1,),
        in_specs=[pl.BlockSpec(a.shape, lambda i, nd=a.ndim: (0,) * nd) for a in args],
        out_specs=pl.BlockSpec((nbt, d), lambda i: (0, 0)),
        out_shape=jax.ShapeDtypeStruct((nbt, d), F32),
        scratch_shapes=[pltpu.VMEM((nbt, 2 * d), BF16), pltpu.VMEM((nbt, d), F32)],
        compiler_params=pltpu.CompilerParams(vmem_limit_bytes=VMEM_LIMIT_SMALL),
        name=name,
    )(*args)


def kernel(x_prompt, x_sample, state_ssm, state_ssd_conv, state_short_conv, meta_tokens, norm_w, w_in,
           conv_ssd_w, conv_ssd_b, dt_bias, a_log, d_skip, ssd_norm_w, conv_sc_w, sc_norm_w, w_out,
           final_norm_w):
    nb, seq, d = x_prompt.shape
    nbt = x_sample.shape[0]
    depth = norm_w.shape[0]
    assert depth == 1 and x_sample.shape[1] == 1
    heads = d // HEADDIM
    d_gn = GROUPS * STATE
    d_xbc = d + 2 * d_gn
    n_meta = meta_tokens.shape[0]
    assert seq % CHUNK == 0 and nbt == CHUNK and n_meta <= CHUNK and heads <= LANES
    nc = seq // CHUNK
    rows_p = nb * seq
    rows = rows_p + CHUNK + nbt

    o_dt = d + d_xbc
    o_sc = o_dt + heads
    assert d % TN == 0 and d_xbc % TN == 0 and (2 * d_gn) == TN and heads == W_ROW_UNIT
    row_starts = ([t * TN for t in range(d // TN)]
                  + [o_sc + t * TN for t in range(4 * d // TN)]
                  + [d + t * TN for t in range(d_xbc // TN)])

    w_t = w_in[0].T
    nw = norm_w[0][None, :]
    cw = conv_ssd_w[0]
    cbias = conv_ssd_b[0][None, :]
    dtb = jnp.pad(dt_bias[0], (0, LANES - heads))[None, :]
    alog = jnp.pad(a_log[0], (0, LANES - heads))[None, :]
    dexp = jnp.repeat(d_skip[0], HEADDIM)[None, :]
    nssd = ssd_norm_w[0][None, :]
    scw = conv_sc_w[0]
    nsc = sc_norm_w[0][None, :]
    nfin = final_norm_w[None, :]
    k_i = lax.broadcasted_iota(jnp.int32, (2 * LANES, d), 0)
    c_i = lax.broadcasted_iota(jnp.int32, (2 * LANES, d), 1)
    e2 = ((k_i % LANES) == (c_i // HEADDIM)).astype(BF16)

    wout32 = _pack_rows(w_out[0], rows_per_step=512, name="pack_wout")

    xp = x_prompt.reshape(rows_p, d)
    x_small = jnp.concatenate(
        [jnp.zeros((CHUNK - n_meta, d), F32), meta_tokens.astype(F32), x_sample[:, 0, :]], axis=0)
    hn32, dt_raw = _norm(xp, x_small, nw, w_t, dt_row0=o_dt, heads=heads, name="norm")
    tm = rows // 8
    assert rows % 8 == 0 and tm % 16 == 0
    p32 = _inproj(hn32, w_t, tm=tm, row_starts=row_starts, name="inproj")

    mixer_wts = (cw, cbias, dtb, alog, dexp, nssd, scw, nsc, wout32, nfin, e2)
    meta_block = rows_p // CHUNK
    sample_block = meta_block + 1

    zeros_state = jnp.zeros((STATE, d), F32)
    _, _, st_t_meta, xt_meta, vt_meta = _mixer(
        p32, dt_raw, x_small, zeros_state, jnp.zeros((8, d_xbc), F32), jnp.zeros((8, d), F32),
        mixer_wts[:8] + (jnp.zeros((8, LANES), U32),) + mixer_wts[9:],
        nb=1, nc=1, row_block0=meta_block, n_pad=CHUNK - n_meta, with_outproj=False, name="mixer_meta")

    cst = jnp.transpose(state_ssd_conv[0], (1, 0, 2))
    sst = state_short_conv[0].reshape(nbt, 2 * d)
    (cst_n, sst_n, xthl, bmat, cmat, dec, decx, xsd, gate, ysc) = _sample_front(
        p32, dt_raw, cst, sst, (cw, cbias, dtb, alog, dexp, scw, nsc, e2),
        row_block=sample_block, d=d, name="sample_front")

    y_p, st_p, _, xt_p, vt_p, st_s, yrow = _mixer(
        p32, dt_raw, xp, st_t_meta[0], xt_meta[0], vt_meta[0],
        mixer_wts, nb=nb, nc=nc, row_block0=0, n_pad=0, with_outproj=True, name="mixer_prompt",
        sample_args=(dec[:, :heads].reshape(-1), state_ssm[0].reshape(nbt, d, STATE), xthl, bmat, cmat))

    y_s = _sample_back(yrow, decx, xsd, gate, ysc, x_sample[:, 0, :], nssd, wout32, nfin,
                       name="sample_back")

    return (
        y_p.reshape(nb, seq, d),
        y_s.reshape(nbt, 1, d),
        st_p.reshape(1, nb, heads, HEADDIM, STATE),
        xt_p[:, 5:8, :][None],
        vt_p[:, 6:8, :][None],
        st_s.reshape(1, nbt, heads, HEADDIM, STATE),
        jnp.transpose(cst_n, (1, 0, 2))[None],
        sst_n.reshape(1, nbt, 2, d),
    )
```

```python
import functools

import jax
import jax.numpy as jnp
from jax import lax
from jax.experimental import pallas as pl
from jax.experimental.pallas import tpu as pltpu

F32 = jnp.float32
BF16 = jnp.bfloat16
U32 = jnp.uint32
EPS = 1e-5
LOG2E = 1.4426950408889634

HEADDIM = 64
GROUPS = 4
STATE = 128
CHUNK = 128
LANES = 128
TW = 256
TN = 1024
NORM_ROWS = 256

VMEM_LIMIT_INPROJ = 52 * 1024 * 1024
VMEM_LIMIT_MIXER = 60 * 1024 * 1024
VMEM_LIMIT_SMALL = 48 * 1024 * 1024


def _silu(x):
    h = 0.5 * x
    return h + h * jnp.tanh(h)


def _softplus(x):
    return jnp.maximum(x, 0.0) + jnp.log1p(jnp.exp(-jnp.abs(x)))


def _split_hi_lo(x):
    hi = x.astype(BF16)
    lo = (x - hi.astype(F32)).astype(BF16)
    return hi, lo


def _pack(x_bf16):
    return pltpu.bitcast(x_bf16, U32)


def _unpack(x_u32):
    return pltpu.bitcast(x_u32, BF16)


def _norm_body(xp_ref, xs_ref, nw_ref, wdt_ref, hn_ref, dt_ref, wdt_s, *, n_prompt_steps, heads):
    i = pl.program_id(0)
    rg = 64

    @pl.when(i == 0)
    def _():
        wdt_s[...] = jnp.zeros_like(wdt_s)
        wdt_s[0:heads, :] = wdt_ref[...].astype(BF16)

    def norm_from(src_ref):
        nw = nw_ref[...]
        for r in range(NORM_ROWS // rg):
            x = src_ref[r * rg:(r + 1) * rg, :]
            ms = jnp.mean(x * x, axis=-1, keepdims=True)
            hn = (x * lax.rsqrt(ms + EPS) * nw).astype(BF16)
            hn_ref[r * (rg // 2):(r + 1) * (rg // 2), :] = _pack(hn)

    @pl.when(i < n_prompt_steps)
    def _():
        norm_from(xp_ref)

    @pl.when(i >= n_prompt_steps)
    def _():
        norm_from(xs_ref)

    dt_ref[...] = lax.dot_general(_unpack(hn_ref[...]), wdt_s[...], (((1,), (1,)), ((), ())),
                                  preferred_element_type=F32)


def _norm(xp, xs, nw, w_t, *, dt_row0, heads, name):
    rows_p, d = xp.shape
    rows_s = xs.shape[0]
    assert rows_p % NORM_ROWS == 0 and rows_s == NORM_ROWS and dt_row0 % heads == 0
    n_prompt_steps = rows_p // NORM_ROWS
    rows = rows_p + rows_s
    return pl.pallas_call(
        functools.partial(_norm_body, n_prompt_steps=n_prompt_steps, heads=heads),
        grid=(n_prompt_steps + 1,),
        in_specs=[
            pl.BlockSpec((NORM_ROWS, d), lambda i: (jnp.minimum(i, n_prompt_steps - 1), 0)),
            pl.BlockSpec((NORM_ROWS, d), lambda i: (0, 0)),
            pl.BlockSpec((1, d), lambda i: (0, 0)),
            pl.BlockSpec((heads, d), lambda i: (dt_row0 // heads, 0)),
        ],
        out_specs=[
            pl.BlockSpec((NORM_ROWS // 2, d), lambda i: (i, 0)),
            pl.BlockSpec((NORM_ROWS, LANES), lambda i: (i, 0)),
        ],
        out_shape=[
            jax.ShapeDtypeStruct((rows // 2, d), U32),
            jax.ShapeDtypeStruct((rows, LANES), F32),
        ],
        scratch_shapes=[pltpu.VMEM((LANES, d), BF16)],
        compiler_params=pltpu.CompilerParams(dimension_semantics=("arbitrary",)),
        name=name,
    )(xp, xs, nw, w_t)


W_ROW_UNIT = 32
PACK_ROWS = 64


def _inproj_body(hn_ref, w_ref, wo_ref, p_ref, wo32_ref, wbf_s, *, n_pack_steps):
    rg = 64

    @pl.when(pl.program_id(0) * pl.num_programs(1) + pl.program_id(1) < n_pack_steps)
    def _():
        wo32_ref[...] = _pack(wo_ref[...].astype(BF16))

    @pl.when(pl.program_id(1) == 0)
    def _():
        def body(r, carry):
            sl = pl.ds(pl.multiple_of(r * rg, rg), rg)
            wbf_s[sl, :] = w_ref[sl, :].astype(BF16)
            return carry

        lax.fori_loop(0, TN // rg, body, 0)

    acc = lax.dot_general(_unpack(hn_ref[...]), wbf_s[...], (((1,), (1,)), ((), ())),
                          preferred_element_type=F32)
    p_ref[...] = _pack(acc.astype(BF16))


def _inproj(hn32, w_t, w_o, *, tm, row_starts, name):
    rows2, d = hn32.shape
    nj = len(row_starts)
    ni = (2 * rows2) // tm
    ko, no = w_o.shape
    assert ko % PACK_ROWS == 0 and ko // PACK_ROWS <= nj * ni
    n_pack_steps = ko // PACK_ROWS
    pmap = lambda j, i, st: (jnp.minimum(j * ni + i, n_pack_steps - 1), 0)
    assert all(r % W_ROW_UNIT == 0 for r in row_starts)
    starts = jnp.asarray([r // W_ROW_UNIT for r in row_starts], jnp.int32)

    grid_spec = pltpu.PrefetchScalarGridSpec(
        num_scalar_prefetch=1,
        grid=(nj, ni),
        in_specs=[
            pl.BlockSpec((tm // 2, d), lambda j, i, st: (i, 0)),
            pl.BlockSpec((pl.Element(TN), pl.Element(d)), lambda j, i, st: (st[j] * W_ROW_UNIT, 0)),
            pl.BlockSpec((PACK_ROWS, no), pmap),
        ],
        out_specs=[pl.BlockSpec((tm // 2, TN), lambda j, i, st: (i, j)),
                   pl.BlockSpec((PACK_ROWS // 2, no), pmap)],
        scratch_shapes=[pltpu.VMEM((TN, d), BF16)],
    )

    def body(st_ref, *refs):
        _inproj_body(*refs, n_pack_steps=n_pack_steps)

    return pl.pallas_call(
        body,
        grid_spec=grid_spec,
        out_shape=[jax.ShapeDtypeStruct((rows2, nj * TN), U32),
                   jax.ShapeDtypeStruct((ko // 2, no), U32)],
        compiler_params=pltpu.CompilerParams(
            dimension_semantics=("arbitrary", "arbitrary"),
            vmem_limit_bytes=VMEM_LIMIT_INPROJ),
        name=name,
    )(starts, hn32, w_t, w_o)


def _conv4(ext, w_ref, cols):
    s1 = pltpu.roll(ext, 1, 0)
    q = ext * w_ref[1:2, cols] + s1 * w_ref[0:1, cols]
    out = ext * w_ref[3:4, cols] + s1 * w_ref[2:3, cols] + pltpu.roll(q, 2, 0)
    return out[8:, :]


def _conv3(ext, w_ref, cols):
    s1 = pltpu.roll(ext, 1, 0)
    q = ext * w_ref[1:2, cols] + s1 * w_ref[0:1, cols]
    out = ext * w_ref[2:3, cols] + pltpu.roll(q, 1, 0)
    return out[8:, :]


def _row_rsqrt_mean_sq(ref, width):
    acc = jnp.zeros((ref.shape[0], LANES), F32)
    for t in range(width // LANES):
        v = ref[:, t * LANES:(t + 1) * LANES]
        acc = acc + v * v
    return lax.rsqrt(jnp.sum(acc, axis=-1, keepdims=True) / width + EPS)


def _sample_state_step(i, spp, rb, dec_ref, st_ref, xthl_ref, bmat_ref, cmat_ref, so_ref, yrow_ref,
                       *, heads):
    nbt = bmat_ref.shape[0]
    d = xthl_ref.shape[0]
    gw = d // GROUPS
    hpg = heads // GROUPS
    rowi = lax.broadcasted_iota(jnp.int32, (nbt, STATE), 0)
    b_rb = pl.multiple_of(((i * spp) // rb) * rb, rb)
    row_rb = lax.broadcasted_iota(jnp.int32, (rb, STATE), 0) + b_rb
    for g in range(GROUPS):
        bm = bmat_ref[:, g * STATE:(g + 1) * STATE]
        cm = cmat_ref[pl.ds(b_rb, rb), g * STATE:(g + 1) * STATE]
        yg = jnp.zeros((rb, gw), F32)
        for q in range(spp):
            b = i * spp + q
            r = jnp.where(rowi == b, bm, jnp.zeros_like(bm))
            upd = jnp.dot(xthl_ref[g * gw:(g + 1) * gw, :], jnp.concatenate([r, r], axis=0),
                          preferred_element_type=F32)
            s_old = st_ref[q, g * gw:(g + 1) * gw, :]
            parts = []
            for hh in range(hpg):
                parts.append(s_old[hh * HEADDIM:(hh + 1) * HEADDIM]
                             * dec_ref[b * heads + g * hpg + hh])
            so_ref[q, g * gw:(g + 1) * gw, :] = jnp.concatenate(parts, axis=0) + upd
            c_b = jnp.where(row_rb == b, cm, 0.0)
            yg = yg + lax.dot_general(c_b.astype(BF16), s_old.astype(BF16),
                                      (((1,), (1,)), ((), ())), preferred_element_type=F32)
        yrow_ref[0, :, g * gw:(g + 1) * gw] = yg


N_MIXER_IN = 17
N_MIXER_OUT = 5
N_SAMPLE_IN = 5
N_SAMPLE_OUT = 2


def _mixer_body(*refs, n_pad, d, heads, nc, total, with_outproj, sample):
    n_in = N_MIXER_IN + (N_SAMPLE_IN if sample else 0)
    n_out = N_MIXER_OUT + (N_SAMPLE_OUT if sample else 0)
    (p_ref, dtr_ref, u_ref,
     s0_ref, xt0_ref, vt0_ref,
     cw_ref, cbias_ref, dtb_ref, alog_ref, dexp_ref, nssd_ref, scw_ref, nsc_ref,
     wout_ref, nfin_ref, e2_ref) = refs[:N_MIXER_IN]
    y_ref, st_ref, stT_ref, xt_ref, vt_ref = refs[n_in:n_in + N_MIXER_OUT]
    (state_s, xh_s, vh_s, xc_s, exp_s, cbm_s, bt_s, at_s, xdt_s, xd_s, ypre_s,
     ytmp_s, ymix_s, ymixp_s, tri_s, nmask_s) = refs[n_in + n_out:]
    s = pl.program_id(0)
    c = s % nc
    d_gn = GROUPS * STATE
    d_xbc = d + 2 * d_gn
    gw = d // GROUPS

    @pl.when(s == 0)
    def _zero():
        ymix_s[...] = jnp.zeros_like(ymix_s)
        r_i = lax.broadcasted_iota(jnp.int32, (CHUNK, CHUNK), 0)
        c_i = lax.broadcasted_iota(jnp.int32, (CHUNK, CHUNK), 1)
        tri_s[...] = (r_i >= c_i).astype(F32)
        nmask_s[...] = jnp.where(r_i >= c_i, 0.0, -jnp.inf)

    @pl.when(c == 0)
    def _init():
        state_s[...] = s0_ref[...]
        xh_s[...] = xt0_ref[...]
        vh_s[...] = vt0_ref[...]

    if with_outproj:
        ymixp_s[...] = ymix_s[...]
    n_pieces = d // TW

    def outproj_piece(n):
        if not with_outproj:
            return
        cols = slice(n * TW, (n + 1) * TW)
        u = u_ref[:, cols] + jnp.dot(ymixp_s[...], _unpack(wout_ref[:, cols]),
                                     preferred_element_type=F32)
        y_ref[:, cols] = u

    if sample:
        spp, rb, n_steps = sample
        _sample_state_step(jnp.minimum(pl.program_id(0), n_steps - 1), spp, rb,
                           *refs[N_MIXER_IN:n_in], *refs[n_in + N_MIXER_OUT:n_in + n_out],
                           heads=heads)

    rowi = lax.broadcasted_iota(jnp.int32, (CHUNK, LANES), 0)
    coli = lax.broadcasted_iota(jnp.int32, (CHUNK, LANES), 1)

    for t in range(d_xbc // TW):
        if t % 5 == 0:
            outproj_piece(t // 5)
        c0 = t * TW
        cols = slice(c0, c0 + TW)
        x = _unpack(p_ref[:, 5 * d + c0:5 * d + c0 + TW]).astype(F32)
        ext = jnp.concatenate([xh_s[:, cols], x], axis=0)
        xc_s[:, cols] = _silu(_conv4(ext, cw_ref, cols) + cbias_ref[0:1, cols])
        xh_s[:, cols] = x[CHUNK - 8:CHUNK, :]

    dtv = dtr_ref[...] + dtb_ref[...]
    dt = _softplus(dtv)
    if n_pad:
        dt = jnp.where(rowi >= n_pad, dt, 0.0)
    dta = dt * (-jnp.exp(alog_ref[...]))
    a_cs = jnp.dot(tri_s[...], dta, precision=lax.Precision.HIGHEST, preferred_element_type=F32)
    at_s[...] = (a_cs * LOG2E).T
    stack = jnp.concatenate(
        [dt, jnp.exp(a_cs), jnp.exp(a_cs[CHUNK - 1:CHUNK, :] - a_cs)], axis=0)
    hi, lo = _split_hi_lo(stack)
    exp_s[...] = jnp.dot(jnp.concatenate([hi, lo], axis=1), e2_ref[...],
                         preferred_element_type=F32)

    for g in range(GROUPS):
        bg = xc_s[:, d + g * STATE:d + (g + 1) * STATE]
        cg = xc_s[:, d + d_gn + g * STATE:d + d_gn + (g + 1) * STATE]
        cbm_s[g] = lax.dot_general(cg.astype(BF16), bg.astype(BF16),
                                   (((1,), (1,)), ((), ())), preferred_element_type=F32)
        bt_s[g] = bg.T.astype(BF16)

    outproj_piece(3)

    for t in range(d // TW):
        cols = slice(t * TW, (t + 1) * TW)
        xdt = xc_s[:, cols] * exp_s[0:CHUNK, cols]
        xdt_s[:, cols] = xdt.astype(BF16)
        xd_s[:, cols] = (xdt * exp_s[2 * CHUNK:3 * CHUNK, cols]).astype(BF16)

    cdec = exp_s[2 * CHUNK - 1:2 * CHUNK, :]
    for g in range(GROUPS):
        cols = slice(g * gw, (g + 1) * gw)
        s_prev = state_s[:, cols]
        cg = xc_s[:, d + d_gn + g * STATE:d + d_gn + (g + 1) * STATE].astype(BF16)
        yoff = jnp.dot(cg, s_prev.astype(BF16), preferred_element_type=F32)
        ypre_s[:, cols] = yoff * exp_s[CHUNK:2 * CHUNK, cols]
        state_s[:, cols] = s_prev * cdec[:, cols] + jnp.dot(
            bt_s[g], xd_s[:, cols], preferred_element_type=F32)

    lane_lo = coli < HEADDIM
    for pr in range(heads // 2):
        if pr % (heads // 4) == 0:
            outproj_piece(4 + pr // (heads // 4))
        g = (2 * pr * HEADDIM) // gw
        ms_ = []
        for hh in (2 * pr, 2 * pr + 1):
            rowb = jnp.broadcast_to(at_s[hh:hh + 1, :], (CHUNK, CHUNK))
            seg = rowb.T - rowb
            decay = jnp.exp2(seg + nmask_s[...])
            ms_.append((cbm_s[g] * decay).astype(BF16))
        lhs = jnp.concatenate(ms_, axis=1)
        cols = slice(pr * LANES, (pr + 1) * LANES)
        xp = xdt_s[:, cols]
        zero = jnp.zeros_like(xp)
        rhs = jnp.concatenate([jnp.where(lane_lo, xp, zero), jnp.where(lane_lo, zero, xp)], axis=0)
        y = jnp.dot(lhs, rhs, preferred_element_type=F32)
        y = y + ypre_s[:, cols] + dexp_ref[0:1, cols] * xc_s[:, cols]
        y = y * _silu(_unpack(p_ref[:, cols]).astype(F32))
        ytmp_s[:, cols] = y
    rs = _row_rsqrt_mean_sq(ytmp_s, d)
    for t in range(d // TW):
        cols = slice(t * TW, (t + 1) * TW)
        ymix_s[:, cols] = (ytmp_s[:, cols] * rs * nssd_ref[0:1, cols]).astype(BF16)

    for t in range(d // TW):
        if t % 4 == 0:
            outproj_piece(6 + t // 4)
        cols = slice(t * TW, (t + 1) * TW)
        pc = lambda k: _unpack(p_ref[:, k * d + t * TW:k * d + (t + 1) * TW]).astype(F32)
        v = pc(3) * pc(4)
        ext = jnp.concatenate([vh_s[:, cols], v], axis=0)
        ysc = pc(2) * _conv3(ext, scw_ref, cols)
        ysc = ysc * _silu(pc(1))
        ytmp_s[:, cols] = ysc
        vh_s[:, cols] = v[CHUNK - 8:CHUNK, :]
    rs = _row_rsqrt_mean_sq(ytmp_s, d)
    for t in range(d // TW):
        cols = slice(t * TW, (t + 1) * TW)
        ymix_s[:, d + t * TW:d + (t + 1) * TW] = (
            ytmp_s[:, cols] * rs * nsc_ref[0:1, cols]).astype(BF16)

    assert n_pieces == 8
    if with_outproj:
        rs = _row_rsqrt_mean_sq(y_ref, d)
        for t in range(d // TW):
            cols = slice(t * TW, (t + 1) * TW)
            y_ref[:, cols] = y_ref[:, cols] * rs * nfin_ref[0:1, cols]
    else:
        y_ref[...] = jnp.zeros_like(y_ref)

    @pl.when(jnp.logical_and(c == nc - 1, s < total))
    def _fin():
        stT_ref[0] = state_s[...]
        for t in range(d // LANES):
            st_ref[0, t * LANES:(t + 1) * LANES, :] = state_s[:, t * LANES:(t + 1) * LANES].T
        xt_ref[0] = xh_s[...]
        vt_ref[0] = vh_s[...]


def _const_spec(shape):
    nd = len(shape)
    return pl.BlockSpec(shape, lambda s: (0,) * nd)


def _mixer(p32, dtr, u, s0, xt0, vt0, wts, *, nb, nc, row_block0, n_pad, with_outproj, name,
           sample_args=None):
    d = u.shape[1]
    d_gn = GROUPS * STATE
    d_xbc = d + 2 * d_gn
    heads = d // HEADDIM
    total = nb * nc
    (cw, cbias, dtb, alog, dexp, nssd, scw, nsc, wout32, nfin, e2) = wts

    rowmap = lambda s: (row_block0 + jnp.minimum(s, total - 1), 0)
    prev = lambda s: (jnp.maximum(s - 1, 0), 0)
    bmap = lambda s: (jnp.minimum(s // nc, nb - 1), 0, 0)

    assert p32.shape[1] == 6 * d + 2 * d_gn
    in_specs = [
        pl.BlockSpec((CHUNK // 2, p32.shape[1]), rowmap),
        pl.BlockSpec((CHUNK, LANES), rowmap),
        pl.BlockSpec((CHUNK, d), prev),
        _const_spec((STATE, d)), _const_spec((8, d_xbc)), _const_spec((8, d)),
        _const_spec(cw.shape), _const_spec(cbias.shape), _const_spec(dtb.shape),
        _const_spec(alog.shape), _const_spec(dexp.shape), _const_spec(nssd.shape),
        _const_spec(scw.shape), _const_spec(nsc.shape),
        pl.BlockSpec(wout32.shape, lambda s: (0, 0), pipeline_mode=pl.Buffered(1)),
        _const_spec(nfin.shape),
        pl.BlockSpec(e2.shape, lambda s: (0, 0), pipeline_mode=pl.Buffered(1)),
    ]
    out_specs = [
        pl.BlockSpec((CHUNK, d), prev),
        pl.BlockSpec((1, d, STATE), bmap),
        pl.BlockSpec((1, STATE, d), bmap),
        pl.BlockSpec((1, 8, d_xbc), bmap),
        pl.BlockSpec((1, 8, d), bmap),
    ]
    out_shape = [
        jax.ShapeDtypeStruct((total * CHUNK, d), F32),
        jax.ShapeDtypeStruct((nb, d, STATE), F32),
        jax.ShapeDtypeStruct((nb, STATE, d), F32),
        jax.ShapeDtypeStruct((nb, 8, d_xbc), F32),
        jax.ShapeDtypeStruct((nb, 8, d), F32),
    ]
    args = [p32, dtr, u, s0, xt0, vt0, cw, cbias, dtb, alog, dexp, nssd,
            scw, nsc, wout32, nfin, e2]
    assert len(args) == N_MIXER_IN and len(in_specs) == N_MIXER_IN
    sample = None
    if sample_args is not None:
        dec_flat, state, xthl, bmat, cmat = sample_args
        nbt = state.shape[0]
        n_grid = total + 1 if with_outproj else total
        spp = 2
        while spp * n_grid < nbt:
            spp *= 2
        rb = max(16, spp)
        assert nbt % rb == 0
        n_steps = nbt // spp
        sample = (spp, rb, n_steps)
        smap = lambda s: (jnp.minimum(s, n_steps - 1), 0, 0)
        in_specs += [
            pl.BlockSpec(memory_space=pltpu.SMEM),
            pl.BlockSpec((spp, d, STATE), smap),
            pl.BlockSpec(xthl.shape, lambda s: (0, 0), pipeline_mode=pl.Buffered(1)),
            _const_spec(bmat.shape), _const_spec(cmat.shape),
        ]
        out_specs += [pl.BlockSpec((spp, d, STATE), smap), pl.BlockSpec((1, rb, d), smap)]
        out_shape += [jax.ShapeDtypeStruct(state.shape, F32),
                      jax.ShapeDtypeStruct((n_steps, rb, d), F32)]
        args += [dec_flat, state, xthl, bmat, cmat]
    scratch = [
        pltpu.VMEM((STATE, d), F32),
        pltpu.VMEM((8, d_xbc), F32),
        pltpu.VMEM((8, d), F32),
        pltpu.VMEM((CHUNK, d_xbc), F32),
        pltpu.VMEM((3 * CHUNK, d), F32),
        pltpu.VMEM((GROUPS, CHUNK, CHUNK), F32),
        pltpu.VMEM((GROUPS, STATE, CHUNK), BF16),
        pltpu.VMEM((LANES, CHUNK), F32),
        pltpu.VMEM((CHUNK, d), BF16),
        pltpu.VMEM((CHUNK, d), BF16),
        pltpu.VMEM((CHUNK, d), F32),
        pltpu.VMEM((CHUNK, d), F32),
        pltpu.VMEM((CHUNK, 2 * d), BF16),
        pltpu.VMEM((CHUNK, 2 * d), BF16),
        pltpu.VMEM((CHUNK, CHUNK), F32),
        pltpu.VMEM((CHUNK, CHUNK), F32),
    ]
    return pl.pallas_call(
        functools.partial(_mixer_body, n_pad=n_pad, d=d, heads=heads, nc=nc, total=total,
                          with_outproj=with_outproj, sample=sample),
        grid=(total + 1 if with_outproj else total,),
        in_specs=in_specs,
        out_specs=out_specs,
        out_shape=out_shape,
        scratch_shapes=scratch,
        compiler_params=pltpu.CompilerParams(
            dimension_semantics=("arbitrary",),
            vmem_limit_bytes=VMEM_LIMIT_MIXER),
        name=name,
    )(*args)


def _sample_front_body(p_ref, dtr_ref, cst_ref, sst_ref,
                       cw_ref, cbias_ref, dtb_ref, alog_ref, dexp_ref, scw_ref, nsc_ref, e2_ref,
                       cst_o, sst_o, xthl_o, bmat_o, cmat_o, dec_o, decx_o, xsd_o, gate_o, ysc_o,
                       xc_s, ytmp_s, *, d):
    d_gn = GROUPS * STATE
    d_xbc = d + 2 * d_gn
    nbt = cst_ref.shape[1]

    for t in range(d_xbc // TW):
        c0 = t * TW
        cols = slice(c0, c0 + TW)
        x = _unpack(p_ref[:, 5 * d + c0:5 * d + c0 + TW]).astype(F32)
        x0 = cst_ref[0, :, cols]
        x1 = cst_ref[1, :, cols]
        x2 = cst_ref[2, :, cols]
        acc = (x0 * cw_ref[0:1, cols] + x1 * cw_ref[1:2, cols] + x2 * cw_ref[2:3, cols]
               + x * cw_ref[3:4, cols] + cbias_ref[0:1, cols])
        xc_s[:, cols] = _silu(acc)
        cst_o[0, :, cols] = x1
        cst_o[1, :, cols] = x2
        cst_o[2, :, cols] = x

    dt = _softplus(dtr_ref[...] + dtb_ref[...])
    dec = jnp.exp(dt * (-jnp.exp(alog_ref[...])))
    dec_o[...] = dec
    hi, lo = _split_hi_lo(jnp.concatenate([dt, dec], axis=0))
    both = jnp.dot(jnp.concatenate([hi, lo], axis=1), e2_ref[...], preferred_element_type=F32)
    dt_exp = both[0:nbt]
    decx_o[...] = both[nbt:2 * nbt]

    bmat_o[...] = xc_s[:, d:d + d_gn].astype(BF16)
    cmat_o[...] = xc_s[:, d + d_gn:d + 2 * d_gn]
    bc = [jnp.sum(xc_s[:, d + g * STATE:d + (g + 1) * STATE]
                  * xc_s[:, d + d_gn + g * STATE:d + d_gn + (g + 1) * STATE], axis=-1, keepdims=True)
          for g in range(GROUPS)]

    for t in range(d // LANES):
        cols = slice(t * LANES, (t + 1) * LANES)
        xs = xc_s[:, cols]
        xdt = xs * dt_exp[:, cols]
        xsd_o[:, cols] = xs * dexp_ref[0:1, cols] + xdt * bc[(t * LANES) // (d // GROUPS)]
        gate_o[:, cols] = _silu(_unpack(p_ref[:, cols]).astype(F32))
        xdt_t = xdt.T
        hi, lo = _split_hi_lo(xdt_t)
        xthl_o[cols, 0:nbt] = hi
        xthl_o[cols, nbt:2 * nbt] = lo

    ss = jnp.zeros((nbt, TW), F32)
    for t in range(d // TW):
        c0 = t * TW
        cols = slice(c0, c0 + TW)
        pc = lambda k: _unpack(p_ref[:, k * d + c0:k * d + c0 + TW]).astype(F32)
        v = pc(3) * pc(4)
        v0 = sst_ref[:, c0:c0 + TW]
        v1 = sst_ref[:, d + c0:d + c0 + TW]
        acc = v0 * scw_ref[0:1, cols] + v1 * scw_ref[1:2, cols] + v * scw_ref[2:3, cols]
        ysc = pc(2) * acc
        ysc = ysc * _silu(pc(1))
        ytmp_s[:, cols] = ysc
        sst_o[:, c0:c0 + TW] = v1
        sst_o[:, d + c0:d + c0 + TW] = v
        ss = ss + ysc * ysc
    rs = lax.rsqrt(jnp.sum(ss, axis=-1, keepdims=True) / d + EPS)
    for t in range(d // TW):
        cols = slice(t * TW, (t + 1) * TW)
        ysc_o[:, cols] = (ytmp_s[:, cols] * rs * nsc_ref[0:1, cols]).astype(BF16)


def _sample_front(p32, dtr, cst, sst, wts, *, row_block, d, name):
    d_gn = GROUPS * STATE
    d_xbc = d + 2 * d_gn
    nbt = cst.shape[1]
    (cw, cbias, dtb, alog, dexp, scw, nsc, e2) = wts
    full = lambda a: pl.BlockSpec(a.shape, lambda i: (0,) * a.ndim)
    in_specs = [
        pl.BlockSpec((nbt // 2, p32.shape[1]), lambda i: (row_block, 0)),
        pl.BlockSpec((nbt, LANES), lambda i: (row_block, 0)),
        full(cst), full(sst),
        full(cw), full(cbias), full(dtb), full(alog), full(dexp), full(scw), full(nsc), full(e2),
    ]
    out_shape = [
        jax.ShapeDtypeStruct(cst.shape, F32),
        jax.ShapeDtypeStruct(sst.shape, F32),
        jax.ShapeDtypeStruct((d, 2 * nbt), BF16),
        jax.ShapeDtypeStruct((nbt, d_gn), BF16),
        jax.ShapeDtypeStruct((nbt, d_gn), F32),
        jax.ShapeDtypeStruct((nbt, LANES), F32),
        jax.ShapeDtypeStruct((nbt, d), F32),
        jax.ShapeDtypeStruct((nbt, d), F32),
        jax.ShapeDtypeStruct((nbt, d), F32),
        jax.ShapeDtypeStruct((nbt, d), BF16),
    ]
    out_specs = [pl.BlockSpec(s.shape, lambda i, nd=len(s.shape): (0,) * nd) for s in out_shape]
    return pl.pallas_call(
        functools.partial(_sample_front_body, d=d),
        grid=(1,),
        in_specs=in_specs,
        out_specs=out_specs,
        out_shape=out_shape,
        scratch_shapes=[pltpu.VMEM((nbt, d_xbc), F32), pltpu.VMEM((nbt, d), F32)],
        compiler_params=pltpu.CompilerParams(vmem_limit_bytes=VMEM_LIMIT_SMALL),
        name=name,
    )(p32, dtr, cst, sst, cw, cbias, dtb, alog, dexp, scw, nsc, e2)


def _sample_back_body(yrow_ref, decx_ref, xsd_ref, gate_ref, ysc_ref, u_ref, nssd_ref, wout_ref,
                      nfin_ref, y_ref, ymix_s, yraw_s, *, d, steps_per_block):
    n_steps, rb, _ = yrow_ref.shape
    for blk in range(n_steps // steps_per_block):
        acc = yrow_ref[blk * steps_per_block]
        for k in range(1, steps_per_block):
            acc = acc + yrow_ref[blk * steps_per_block + k]
        yraw_s[blk * rb:(blk + 1) * rb, :] = acc
    y = (decx_ref[...] * yraw_s[...] + xsd_ref[...]) * gate_ref[...]
    rs = lax.rsqrt(jnp.mean(y * y, axis=-1, keepdims=True) + EPS)
    ymix_s[:, 0:d] = (y * rs * nssd_ref[...]).astype(BF16)
    ymix_s[:, d:2 * d] = ysc_ref[...]
    out = jnp.dot(ymix_s[...], _unpack(wout_ref[...]), preferred_element_type=F32)
    u = u_ref[...] + out
    ms = jnp.mean(u * u, axis=-1, keepdims=True)
    y_ref[...] = u * lax.rsqrt(ms + EPS) * nfin_ref[...]


def _sample_back(yrow, decx, xsd, gate, ysc, u, nssd, wout32, nfin, *, name):
    nbt, d = u.shape
    n_steps, rb, _ = yrow.shape
    args = (yrow, decx, xsd, gate, ysc, u, nssd, wout32, nfin)
    return pl.pallas_call(
        functools.partial(_sample_back_body, d=d, steps_per_block=(n_steps * rb) // nbt),
        grid=(1,),
        in_specs=[pl.BlockSpec(a.shape, lambda i, nd=a.ndim: (0,) * nd) for a in args],
        out_specs=pl.BlockSpec((nbt, d), lambda i: (0, 0)),
        out_shape=jax.ShapeDtypeStruct((nbt, d), F32),
        scratch_shapes=[pltpu.VMEM((nbt, 2 * d), BF16), pltpu.VMEM((nbt, d), F32)],
        compiler_params=pltpu.CompilerParams(vmem_limit_bytes=VMEM_LIMIT_SMALL),
        name=name,
    )(*args)


def kernel(x_prompt, x_sample, state_ssm, state_ssd_conv, state_short_conv, meta_tokens, norm_w, w_in,
           conv_ssd_w, conv_ssd_b, dt_bias, a_log, d_skip, ssd_norm_w, conv_sc_w, sc_norm_w, w_out,
           final_norm_w):
    nb, seq, d = x_prompt.shape
    nbt = x_sample.shape[0]
    depth = norm_w.shape[0]
    assert depth == 1 and x_sample.shape[1] == 1
    heads = d // HEADDIM
    d_gn = GROUPS * STATE
    d_xbc = d + 2 * d_gn
    n_meta = meta_tokens.shape[0]
    assert seq % CHUNK == 0 and nbt == CHUNK and n_meta <= CHUNK and heads <= LANES
    nc = seq // CHUNK
    rows_p = nb * seq
    rows = rows_p + CHUNK + nbt

    o_dt = d + d_xbc
    o_sc = o_dt + heads
    assert d % TN == 0 and d_xbc % TN == 0 and (2 * d_gn) == TN and heads == W_ROW_UNIT
    row_starts = ([t * TN for t in range(d // TN)]
                  + [o_sc + t * TN for t in range(4 * d // TN)]
                  + [d + t * TN for t in range(d_xbc // TN)])

    w_t = w_in[0].T
    nw = norm_w[0][None, :]
    cw = conv_ssd_w[0]
    cbias = conv_ssd_b[0][None, :]
    dtb = jnp.pad(dt_bias[0], (0, LANES - heads))[None, :]
    alog = jnp.pad(a_log[0], (0, LANES - heads))[None, :]
    dexp = jnp.repeat(d_skip[0], HEADDIM)[None, :]
    nssd = ssd_norm_w[0][None, :]
    scw = conv_sc_w[0]
    nsc = sc_norm_w[0][None, :]
    nfin = final_norm_w[None, :]
    k_i = lax.broadcasted_iota(jnp.int32, (2 * LANES, d), 0)
    c_i = lax.broadcasted_iota(jnp.int32, (2 * LANES, d), 1)
    e2 = ((k_i % LANES) == (c_i // HEADDIM)).astype(BF16)

    xp = x_prompt.reshape(rows_p, d)
    x_small = jnp.concatenate(
        [jnp.zeros((CHUNK - n_meta, d), F32), meta_tokens.astype(F32), x_sample[:, 0, :]], axis=0)
    hn32, dt_raw = _norm(xp, x_small, nw, w_t, dt_row0=o_dt, heads=heads, name="norm")
    n_row_tiles = 6
    tm = rows // n_row_tiles
    assert rows % n_row_tiles == 0 and tm % 16 == 0
    p32, wout32 = _inproj(hn32, w_t, w_out[0], tm=tm, row_starts=row_starts, name="inproj")

    mixer_wts = (cw, cbias, dtb, alog, dexp, nssd, scw, nsc, wout32, nfin, e2)
    meta_block = rows_p // CHUNK
    sample_block = meta_block + 1

    zeros_state = jnp.zeros((STATE, d), F32)
    _, _, st_t_meta, xt_meta, vt_meta = _mixer(
        p32, dt_raw, x_small, zeros_state, jnp.zeros((8, d_xbc), F32), jnp.zeros((8, d), F32),
        mixer_wts[:8] + (jnp.zeros((8, LANES), U32),) + mixer_wts[9:],
        nb=1, nc=1, row_block0=meta_block, n_pad=CHUNK - n_meta, with_outproj=False, name="mixer_meta")

    cst = jnp.transpose(state_ssd_conv[0], (1, 0, 2))
    sst = state_short_conv[0].reshape(nbt, 2 * d)
    (cst_n, sst_n, xthl, bmat, cmat, dec, decx, xsd, gate, ysc) = _sample_front(
        p32, dt_raw, cst, sst, (cw, cbias, dtb, alog, dexp, scw, nsc, e2),
        row_block=sample_block, d=d, name="sample_front")

    y_p, st_p, _, xt_p, vt_p, st_s, yrow = _mixer(
        p32, dt_raw, xp, st_t_meta[0], xt_meta[0], vt_meta[0],
        mixer_wts, nb=nb, nc=nc, row_block0=0, n_pad=0, with_outproj=True, name="mixer_prompt",
        sample_args=(dec[:, :heads].reshape(-1), state_ssm[0].reshape(nbt, d, STATE), xthl, bmat, cmat))

    y_s = _sample_back(yrow, decx, xsd, gate, ysc, x_sample[:, 0, :], nssd, wout32, nfin,
                       name="sample_back")

    return (
        y_p.reshape(nb, seq, d),
        y_s.reshape(nbt, 1, d),
        st_p.reshape(1, nb, heads, HEADDIM, STATE),
        xt_p[:, 5:8, :][None],
        vt_p[:, 6:8, :][None],
        st_s.reshape(1, nbt, heads, HEADDIM, STATE),
        jnp.transpose(cst_n, (1, 0, 2))[None],
        sst_n.reshape(1, nbt, 2, d),
    )
```
